```python
import math
import jax, jax.numpy as jnp
from jax import lax
import numpy as np

D_MODEL = 2048
BATCH = 4
SEQ = 2048
DEPTH = 1
DEC_BATCH = 32
DEC_SEQ = 1
PAST_LEN = 16384
PAGE_SIZE = 128

HEAD_DIM = 128
GMLP_WIDTH = D_MODEL // 4
GMLP_CH = 128
GMLP_GROUPS = GMLP_WIDTH // GMLP_CH
ATTN_WIDTH = D_MODEL - GMLP_WIDTH
ATTN_HEADS = ATTN_WIDTH // HEAD_DIM
DILATION_GROUPS = ((128, 1), (512, 4), (2048, 16))
N_DIL = len(DILATION_GROUPS)
HPG = ATTN_HEADS // N_DIL
IN_WIDTH = 3 * ATTN_WIDTH + 2 * GMLP_WIDTH
CHUNK = 128
QBLK = 128
ROT_DIM = HEAD_DIM // 4
ROPE_THETA = 500000.0
N_EXPERTS = 32
TOP_K = 4
D_FF = D_MODEL
SWIGLU_LIMIT = 7.0
SWIGLU_ALPHA = 1.702
MOE_BLOCK = 128
PLE_DIM = 256
EPS = 1e-6

kernel_name = "hymba_dilated_gmlp_moe_step"


def rmsnorm(x, g):
    xf = x.astype(jnp.float32)
    y = xf * lax.rsqrt(jnp.mean(xf * xf, axis=-1, keepdims=True) + EPS)
    return (y * g.astype(jnp.float32)).astype(x.dtype)


def layernorm(x, g, b):
    xf = x.astype(jnp.float32)
    mu = jnp.mean(xf, axis=-1, keepdims=True)
    var = jnp.mean(jnp.square(xf - mu), axis=-1, keepdims=True)
    y = (xf - mu) * lax.rsqrt(var + EPS)
    return (y * g.astype(jnp.float32) + b.astype(jnp.float32)).astype(x.dtype)


def rope_partial(x, pos):
    half = ROT_DIM // 2
    inv = 1.0 / (ROPE_THETA ** (jnp.arange(half, dtype=jnp.float32) * (2.0 / ROT_DIM)))
    ang = pos.astype(jnp.float32)[:, None] * inv[None, :]
    cos = jnp.cos(ang)[None, :, None, :]
    sin = jnp.sin(ang)[None, :, None, :]
    xr = x[..., :ROT_DIM].astype(jnp.float32)
    x1, x2 = xr[..., :half], xr[..., half:]
    rot = jnp.concatenate([x1 * cos - x2 * sin, x2 * cos + x1 * sin], axis=-1)
    return jnp.concatenate([rot.astype(x.dtype), x[..., ROT_DIM:]], axis=-1)


def project(h, g_mix, w_in, g_q, g_k, pos):
    B, T, _ = h.shape
    z = rmsnorm(h, g_mix) @ w_in
    q, k, v, u, vg = jnp.split(z, [ATTN_WIDTH, 2 * ATTN_WIDTH, 3 * ATTN_WIDTH,
                                   3 * ATTN_WIDTH + GMLP_WIDTH], axis=-1)
    q = rope_partial(rmsnorm(q.reshape(B, T, ATTN_HEADS, HEAD_DIM), g_q), pos)
    k = rope_partial(rmsnorm(k.reshape(B, T, ATTN_HEADS, HEAD_DIM), g_k), pos)
    v = v.reshape(B, T, ATTN_HEADS, HEAD_DIM)
    return q, k, v, u, vg


def dilated_band_attention(q, k, v, dilation, n_steps):
    B, S, H, D = q.shape
    assert n_steps <= QBLK
    L = S // dilation
    nb = -(-L // QBLK)
    Lp = nb * QBLK
    N = B * dilation

    def classes(x):
        x = x.reshape(B, L, dilation, H, D).transpose(0, 2, 1, 3, 4).reshape(N, L, H, D)
        x = jnp.pad(x, ((0, 0), (0, Lp - L), (0, 0), (0, 0)))
        return x.reshape(N, nb, QBLK, H, D)

    def band(x):
        prev = jnp.pad(x[:, :-1], ((0, 0), (1, 0), (0, 0), (0, 0), (0, 0)))
        return jnp.concatenate([prev, x], axis=2)

    qb = classes(q)
    kband, vband = band(classes(k)), band(classes(v))
    s = jnp.einsum('nbqhd,nbkhd->nbhqk', qb, kband).astype(jnp.float32) * (D ** -0.5)
    qi = jnp.arange(QBLK)[:, None]
    kj = jnp.arange(2 * QBLK)[None, :]
    dist = QBLK + qi - kj
    key_idx = (jnp.arange(nb)[:, None, None] - 1) * QBLK + kj[None]
    mask = (dist >= 0)[None] & (dist <= n_steps)[None] & (key_idx >= 0)
    s = jnp.where(mask[None, :, None], s, -jnp.inf)
    m = jnp.max(s, axis=-1, keepdims=True)
    p = jnp.exp(s - m)
    den = jnp.sum(p, axis=-1, keepdims=True)
    o = jnp.einsum('nbhqk,nbkhd->nbqhd', (p / den).astype(v.dtype), vband)
    lse = (m + jnp.log(den))[..., 0]
    o = o.reshape(B, dilation, Lp, H, D)[:, :, :L].transpose(0, 2, 1, 3, 4).reshape(B, S, H, D)
    lse = lse.transpose(0, 1, 3, 2).reshape(B, dilation, Lp, H)[:, :, :L]
    lse = lse.transpose(0, 2, 1, 3).reshape(B, S, H)
    return o, lse


def dilated_window_decode(q, k_new, v_new, cache_k, cache_v, dilation, n_steps):
    B, T, H, D = q.shape
    Wc = cache_k.shape[1]
    kw = jnp.concatenate([cache_k, k_new], axis=1)
    vw = jnp.concatenate([cache_v, v_new], axis=1)
    idx = Wc + jnp.arange(T)[:, None] - dilation * jnp.arange(n_steps + 1)[None, :]
    valid = idx >= 0
    idx = jnp.maximum(idx, 0)
    kg, vg = kw[:, idx], vw[:, idx]
    s = jnp.einsum('bthd,btjhd->bhtj', q, kg).astype(jnp.float32) * (D ** -0.5)
    s = jnp.where(valid[None, None], s, -jnp.inf)
    m = jnp.max(s, axis=-1, keepdims=True)
    p = jnp.exp(s - m)
    den = jnp.sum(p, axis=-1, keepdims=True)
    o = jnp.einsum('bhtj,btjhd->bthd', (p / den).astype(v_new.dtype), vg)
    lse = (m + jnp.log(den))[..., 0].transpose(0, 2, 1)
    return o, lse


def merge_dilations(outs, lses):
    o = jnp.stack(outs, axis=0)
    alpha = jax.nn.softmax(jnp.stack(lses, axis=0), axis=0)
    o = o * alpha[..., None].astype(o.dtype)
    B, T = o.shape[1], o.shape[2]
    return jnp.moveaxis(o, 0, 2).reshape(B, T, ATTN_WIDTH)


def spatial_gate(u, vg, ln_g, ln_b, w_s, b_s):
    B, T, _ = u.shape
    u = jax.nn.gelu(u)
    vn = layernorm(jax.nn.gelu(vg).reshape(B, T, GMLP_GROUPS, GMLP_CH), ln_g, ln_b)
    c = min(T, CHUNK)
    nc = T // c
    causal = jnp.tril(jnp.ones((c, c), dtype=bool))
    ws = jnp.where(causal[None], w_s[:, :c, :c], 0.0)
    mixed = jnp.einsum('gij,bnjgc->bnigc', ws, vn.reshape(B, nc, c, GMLP_GROUPS, GMLP_CH))
    mixed = mixed + b_s[:, :c].T[None, None, :, :, None]
    return u * mixed.reshape(B, T, GMLP_WIDTH), vn


def mix_out(attn, gm, g_out_attn, g_out_gmlp, w_out):
    y = jnp.concatenate([rmsnorm(attn, g_out_attn), rmsnorm(gm, g_out_gmlp)], axis=-1)
    return y @ w_out


def moe_ffn(x, w_router, b_router, w_gu, b_gu, w_down, b_down):
    T, D = x.shape
    logits = (x @ w_router).astype(jnp.float32) + b_router.astype(jnp.float32)
    top_val, top_idx = lax.top_k(logits, TOP_K)
    gates = jax.nn.softmax(top_val, axis=-1)
    A = T * TOP_K
    blk = int(min(MOE_BLOCK, max(8, 1 << int(math.log2(max(A // N_EXPERTS, 1))))))
    n_blocks = -(-A // blk) + N_EXPERTS
    flat_e = top_idx.reshape(A)
    order = jnp.argsort(flat_e)
    e_sorted = flat_e[order]
    tok_sorted = order // TOP_K
    counts = jnp.zeros((N_EXPERTS,), jnp.int32).at[flat_e].add(1)
    padded = (counts + blk - 1) // blk * blk
    start = jnp.cumsum(counts) - counts
    pend = jnp.cumsum(padded)
    pstart = pend - padded
    dest = pstart[e_sorted] + jnp.arange(A) - start[e_sorted]
    rows = jnp.zeros((n_blocks * blk, D), x.dtype).at[dest].set(x[tok_sorted])
    block_expert = jnp.clip(jnp.searchsorted(pend, jnp.arange(n_blocks) * blk, side='right'),
                            0, N_EXPERTS - 1)

    def expert_block(args):
        xb, e = args
        gu = xb @ w_gu[e] + b_gu[e]
        gate = jnp.minimum(gu[:, :D_FF], SWIGLU_LIMIT)
        lin = jnp.clip(gu[:, D_FF:], -SWIGLU_LIMIT, SWIGLU_LIMIT)
        hdn = gate * jax.nn.sigmoid(SWIGLU_ALPHA * gate) * (lin + 1.0)
        return hdn @ w_down[e] + b_down[e]

    ys = lax.map(expert_block, (rows.reshape(n_blocks, blk, D), block_expert))
    y_sorted = ys.reshape(n_blocks * blk, D)[dest] * gates.reshape(A)[order][:, None].astype(x.dtype)
    return jax.ops.segment_sum(y_sorted, tok_sorted, num_segments=T)


def ffn_and_ple(h, p, g_ffn, w_router, b_router, w_gu, b_gu, w_down, b_down,
                g_ple, w_ple_gate, w_ple_proj):
    B, T, D = h.shape
    m = rmsnorm(h, g_ffn).reshape(B * T, D)
    h = h + moe_ffn(m, w_router, b_router, w_gu, b_gu, w_down, b_down).reshape(B, T, D)
    gate = jax.nn.sigmoid(rmsnorm(h, g_ple) @ w_ple_gate)
    return h + gate * (p @ w_ple_proj)


def setup_inputs(seed: int = 0) -> dict:
    key = jax.random.key(seed)
    ks = iter(jax.random.split(key, 40))
    f32 = jnp.float32

    def nrm(shape, scale=1.0):
        return jax.random.normal(next(ks), shape, f32) * scale

    def gain(shape):
        return 1.0 + 0.05 * nrm(shape)

    def kv_cache(window):
        return nrm((DEPTH, DEC_BATCH, min(window, PAST_LEN), 2, HPG, HEAD_DIM))

    return {
        "x_prompt": nrm((BATCH, SEQ, D_MODEL)),
        "x_sample": nrm((DEC_BATCH, DEC_SEQ, D_MODEL)),
        "cache_kv_w128": kv_cache(DILATION_GROUPS[0][0]),
        "cache_kv_w512": kv_cache(DILATION_GROUPS[1][0]),
        "cache_kv_w2048": kv_cache(DILATION_GROUPS[2][0]),
        "p_prompt": nrm((DEPTH, BATCH, SEQ, PLE_DIM)),
        "p_sample": nrm((DEPTH, DEC_BATCH, DEC_SEQ, PLE_DIM)),
        "g_mix": gain((DEPTH, D_MODEL)),
        "w_in": nrm((DEPTH, D_MODEL, IN_WIDTH), D_MODEL ** -0.5),
        "g_q": gain((DEPTH, HEAD_DIM)),
        "g_k": gain((DEPTH, HEAD_DIM)),
        "gmlp_ln_g": gain((DEPTH, GMLP_GROUPS, GMLP_CH)),
        "gmlp_ln_b": nrm((DEPTH, GMLP_GROUPS, GMLP_CH), 0.02),
        "gmlp_w_s": nrm((DEPTH, GMLP_GROUPS, CHUNK, CHUNK), CHUNK ** -0.5),
        "gmlp_b_s": 1.0 + nrm((DEPTH, GMLP_GROUPS, CHUNK), 0.05),
        "g_out_attn": gain((DEPTH, ATTN_WIDTH)),
        "g_out_gmlp": gain((DEPTH, GMLP_WIDTH)),
        "w_out": nrm((DEPTH, D_MODEL, D_MODEL), D_MODEL ** -0.5),
        "g_ffn": gain((DEPTH, D_MODEL)),
        "w_router": nrm((DEPTH, D_MODEL, N_EXPERTS), D_MODEL ** -0.5),
        "b_router": nrm((DEPTH, N_EXPERTS), 0.01),
        "w_gu": nrm((DEPTH, N_EXPERTS, D_MODEL, 2 * D_FF), D_MODEL ** -0.5),
        "b_gu": nrm((DEPTH, N_EXPERTS, 2 * D_FF), 0.02),
        "w_down": nrm((DEPTH, N_EXPERTS, D_FF, D_MODEL), D_FF ** -0.5),
        "b_down": nrm((DEPTH, N_EXPERTS, D_MODEL), 0.02),
        "g_ple": gain((DEPTH, D_MODEL)),
        "w_ple_gate": nrm((DEPTH, D_MODEL, D_MODEL), D_MODEL ** -0.5),
        "w_ple_proj": nrm((DEPTH, PLE_DIM, D_MODEL), PLE_DIM ** -0.5),
    }


def reference(x_prompt, x_sample, cache_kv_w128, cache_kv_w512, cache_kv_w2048, p_prompt, p_sample,
              g_mix, w_in, g_q, g_k, gmlp_ln_g, gmlp_ln_b, gmlp_w_s, gmlp_b_s, g_out_attn, g_out_gmlp,
              w_out, g_ffn, w_router, b_router, w_gu, b_gu, w_down, b_down, g_ple, w_ple_gate, w_ple_proj):
    S = x_prompt.shape[1]
    T = x_sample.shape[1]
    pos_p = jnp.arange(S, dtype=jnp.int32)
    pos_s = PAST_LEN + jnp.arange(T, dtype=jnp.int32)
    caches = (cache_kv_w128, cache_kv_w512, cache_kv_w2048)
    hp, hs = x_prompt, x_sample
    new_p = ([], [], [])
    new_s = ([], [], [])
    gmlp_rows = []
    for i in range(DEPTH):
        q, k, v, u, vg = project(hp, g_mix[i], w_in[i], g_q[i], g_k[i], pos_p)
        outs, lses = [], []
        for g, (window, dil) in enumerate(DILATION_GROUPS):
            sl = slice(g * HPG, (g + 1) * HPG)
            o, l = dilated_band_attention(q[:, :, sl], k[:, :, sl], v[:, :, sl], dil, window // dil)
            outs.append(o)
            lses.append(l)
            wc = min(window, S)
            new_p[g].append(jnp.stack([k[:, S - wc:, sl], v[:, S - wc:, sl]], axis=2))
        gm, _ = spatial_gate(u, vg, gmlp_ln_g[i], gmlp_ln_b[i], gmlp_w_s[i], gmlp_b_s[i])
        hp = hp + mix_out(merge_dilations(outs, lses), gm, g_out_attn[i], g_out_gmlp[i], w_out[i])
        hp = ffn_and_ple(hp, p_prompt[i], g_ffn[i], w_router[i], b_router[i], w_gu[i], b_gu[i],
                         w_down[i], b_down[i], g_ple[i], w_ple_gate[i], w_ple_proj[i])

        q, k, v, u, vg = project(hs, g_mix[i], w_in[i], g_q[i], g_k[i], pos_s)
        outs, lses = [], []
        for g, (window, dil) in enumerate(DILATION_GROUPS):
            sl = slice(g * HPG, (g + 1) * HPG)
            c = caches[g][i]
            o, l = dilated_window_decode(q[:, :, sl], k[:, :, sl], v[:, :, sl],
                                         c[:, :, 0], c[:, :, 1], dil, window // dil)
            outs.append(o)
            lses.append(l)
            new_s[g].append(jnp.stack([k[:, :, sl], v[:, :, sl]], axis=2))
        gm, vn = spatial_gate(u, vg, gmlp_ln_g[i], gmlp_ln_b[i], gmlp_w_s[i], gmlp_b_s[i])
        gmlp_rows.append(vn)
        hs = hs + mix_out(merge_dilations(outs, lses), gm, g_out_attn[i], g_out_gmlp[i], w_out[i])
        hs = ffn_and_ple(hs, p_sample[i], g_ffn[i], w_router[i], b_router[i], w_gu[i], b_gu[i],
                         w_down[i], b_down[i], g_ple[i], w_ple_gate[i], w_ple_proj[i])
    kv_w128_prompt = jnp.stack(new_p[0], axis=0)
    kv_w512_prompt = jnp.stack(new_p[1], axis=0)
    kv_w2048_prompt = jnp.stack(new_p[2], axis=0)
    kv_w128_sample = jnp.stack(new_s[0], axis=0)
    kv_w512_sample = jnp.stack(new_s[1], axis=0)
    kv_w2048_sample = jnp.stack(new_s[2], axis=0)
    gmlp_v_sample = jnp.stack(gmlp_rows, axis=0)
    return (hp, hs, kv_w128_prompt, kv_w512_prompt, kv_w2048_prompt,
            kv_w128_sample, kv_w512_sample, kv_w2048_sample, gmlp_v_sample)
```

```python
import functools

import jax
import jax.numpy as jnp
from jax import lax
from jax.experimental import pallas as pl
from jax.experimental.pallas import tpu as pltpu

F32 = jnp.float32
BF16 = jnp.bfloat16

EPS = 1e-6
HEAD_DIM = 128
ROT_DIM = HEAD_DIM // 4
ROPE_THETA = 500000.0
DILATION_GROUPS = ((128, 1), (512, 4), (2048, 16))
PAST_LEN = 16384
QBLK = 128
CHUNK = 128
GMLP_CH = 128
TOP_K = 4
SWIGLU_LIMIT = 7.0
SWIGLU_ALPHA = 1.702

V7X_VMEM_BYTES = 64 * 1024 * 1024
VMEM_LIMIT = V7X_VMEM_BYTES - 8 * 1024 * 1024

MOE_CHUNK = 256
MOE_MAX_CHUNKS = 5
MOE_FF_TILE = 256


def _params(*sem):
    return pltpu.CompilerParams(dimension_semantics=sem, vmem_limit_bytes=VMEM_LIMIT)


def _split_bf16(a):
    hi = a.astype(BF16)
    lo = (a - hi.astype(F32)).astype(BF16)
    return hi, lo


def _dot3(a, b):
    ah, al = _split_bf16(a)
    bh, bl = _split_bf16(b)
    d = functools.partial(jnp.dot, preferred_element_type=F32)
    return d(ah, bh) + (d(ah, bl) + d(al, bh))


def _gelu(x):
    return 0.5 * x * (1.0 + jnp.tanh(0.7978845608028654 * (x + 0.044715 * (x * x * x))))


def _rms_scale(x):
    return lax.rsqrt(jnp.mean(x * x, axis=-1, keepdims=True) + EPS)


def _norm_matmul_kernel(x_ref, g_ref, w_ref, o_ref, xn_ref, *, exact):
    @pl.when(pl.program_id(1) == 0)
    def _():
        rows = min(x_ref.shape[0], 128)

        def body(c, carry):
            sl = pl.ds(pl.multiple_of(c * rows, rows), rows)
            x = x_ref[sl, :]
            xn_ref[sl, :] = (x * _rms_scale(x) * g_ref[...]).astype(xn_ref.dtype)
            return carry

        lax.fori_loop(0, x_ref.shape[0] // rows, body, 0)

    if exact:
        o_ref[...] = _dot3(xn_ref[...], w_ref[...])
    else:
        o_ref[...] = jnp.dot(xn_ref[...], w_ref[...].astype(BF16), preferred_element_type=F32)


def _norm_matmul(x, g, w, *, tm, tn, exact, name):
    m, k = x.shape
    n = w.shape[1]
    return pl.pallas_call(
        functools.partial(_norm_matmul_kernel, exact=exact),
        grid=(m // tm, n // tn),
        in_specs=[
            pl.BlockSpec((tm, k), lambda i, j: (i, 0)),
            pl.BlockSpec((1, k), lambda i, j: (0, 0)),
            pl.BlockSpec((k, tn), lambda i, j: (0, j)),
        ],
        out_specs=pl.BlockSpec((tm, tn), lambda i, j: (i, j)),
        out_shape=jax.ShapeDtypeStruct((m, n), F32),
        scratch_shapes=[pltpu.VMEM((tm, k), F32 if exact else BF16)],
        compiler_params=_params("parallel", "arbitrary"),
        name=name,
    )(x, g.reshape(1, k), w)


def _rope_tables(pos):
    half = ROT_DIM // 2
    inv = 1.0 / (ROPE_THETA ** (jnp.arange(half, dtype=F32) * (2.0 / ROT_DIM)))
    ang = pos.astype(F32)[:, None] * inv[None, :]
    cos, sin = jnp.cos(ang), jnp.sin(ang)
    t = pos.shape[0]
    pad = HEAD_DIM - ROT_DIM
    cos_t = jnp.concatenate([cos, cos, jnp.ones((t, pad), F32)], axis=1)
    sa = jnp.concatenate([-sin, jnp.zeros((t, HEAD_DIM - half), F32)], axis=1)
    sb = jnp.concatenate([jnp.zeros((t, half), F32), sin, jnp.zeros((t, pad), F32)], axis=1)
    return cos_t, sa, sb


def _norm_rope(x, gain, cos_t, sa, sb):
    half = ROT_DIM // 2
    xn = x * _rms_scale(x) * gain
    up = pltpu.roll(xn, HEAD_DIM - half, 1)
    dn = pltpu.roll(xn, half, 1)
    return xn * cos_t + up * sa + dn * sb


def _band_block(q, k, v, mask):
    s = lax.dot_general(q.astype(BF16), k.astype(BF16), (((1,), (1,)), ((), ())),
                        preferred_element_type=F32)
    s = jnp.where(mask, s, -jnp.inf)
    m = jnp.max(s, axis=-1, keepdims=True)
    p = jnp.exp(s - m)
    den = jnp.sum(p, axis=-1, keepdims=True)
    o = jnp.dot((p / den).astype(BF16), v.astype(BF16), preferred_element_type=F32)
    return o, m + jnp.log(den)


def _attn_kernel(*refs, seq, groups):
    ng = len(groups)
    qkv = refs[:3 * ng]
    cos_ref, sa_ref, sb_ref, gq_ref, gk_ref = refs[3 * ng:3 * ng + 5]
    outs = refs[3 * ng + 5:3 * ng + 5 + 2 * ng]
    o_refs, kn_refs = outs[:ng], outs[ng:]
    qs, ks = refs[3 * ng + 5 + 2 * ng:3 * ng + 7 + 2 * ng]
    ls_refs = refs[3 * ng + 7 + 2 * ng:]
    scale = HEAD_DIM ** -0.5
    nrm_rows = 256

    qi = lax.broadcasted_iota(jnp.int32, (QBLK, 2 * QBLK), 0)
    kj = lax.broadcasted_iota(jnp.int32, (QBLK, 2 * QBLK), 1)
    qi1 = lax.broadcasted_iota(jnp.int32, (QBLK, QBLK), 0)
    kj1 = lax.broadcasted_iota(jnp.int32, (QBLK, QBLK), 1)

    for g, (window, dil) in enumerate(groups):
        q_ref, k_ref, v_ref = qkv[3 * g:3 * g + 3]
        o_ref, kn_ref, ls_ref = o_refs[g], kn_refs[g], ls_refs[g]
        n_steps = window // dil
        nb = seq // dil // QBLK

        def prep(c, carry, q_ref=q_ref, k_ref=k_ref, kn_ref=kn_ref):
            sl = pl.ds(pl.multiple_of(c * nrm_rows, nrm_rows), nrm_rows)
            tabs = (cos_ref[sl, :], sa_ref[sl, :], sb_ref[sl, :])
            qs[sl, :] = _norm_rope(q_ref[sl, :], gq_ref[...], *tabs) * scale
            kn = _norm_rope(k_ref[sl, :], gk_ref[...], *tabs)
            kn_ref[sl, :] = kn
            ks[sl, :] = kn
            return carry

        lax.fori_loop(0, seq // nrm_rows, prep, 0)

        def rows(start, size, dil=dil):
            return pl.ds(start, size) if dil == 1 else pl.ds(start, size, stride=dil)

        band_mask = (kj >= qi + (QBLK - n_steps)) & (kj <= qi + QBLK)
        first_mask = (kj1 <= qi1) & (qi1 - kj1 <= n_steps)

        def one_class(r, carry, v_ref=v_ref, o_ref=o_ref, ls_ref=ls_ref, rows=rows,
                      band_mask=band_mask, first_mask=first_mask, dil=dil, nb=nb):
            sl0 = rows(r, QBLK)
            o, lse = _band_block(qs[sl0, :], ks[sl0, :], v_ref[sl0, :], first_mask)
            o_ref[sl0, :] = o
            ls_ref[sl0, :] = jnp.broadcast_to(lse, (QBLK, HEAD_DIM))

            def one_block(lb, carry2):
                q0 = r + dil * QBLK * lb
                k0 = q0 - dil * QBLK
                slq, slk = rows(q0, QBLK), rows(k0, 2 * QBLK)
                o, lse = _band_block(qs[slq, :], ks[slk, :], v_ref[slk, :], band_mask)
                o_ref[slq, :] = o
                ls_ref[slq, :] = jnp.broadcast_to(lse, (QBLK, HEAD_DIM))
                return carry2

            lax.fori_loop(1, nb, one_block, 0)
            return carry

        lax.fori_loop(0, dil, one_class, 0)

    def merge(c, carry):
        sl = pl.ds(pl.multiple_of(c * nrm_rows, nrm_rows), nrm_rows)
        ls = [ls_refs[g][sl, :] for g in range(ng)]
        mx = functools.reduce(jnp.maximum, ls)
        es = [jnp.exp(l - mx) for l in ls]
        inv = 1.0 / functools.reduce(lambda a, b: a + b, es)
        for g in range(ng):
            o_refs[g][sl, :] = o_refs[g][sl, :] * (es[g] * inv)
        return carry

    lax.fori_loop(0, seq // nrm_rows, merge, 0)


def _prompt_attention(z, g_q, g_k, tables, *, batch, seq, hpg):
    ng = len(DILATION_GROUPS)
    heads = ng * hpg
    blk = (None, seq, HEAD_DIM)

    def col(c):
        return pl.BlockSpec(blk, lambda b, h, c=c: (b, 0, c + h))

    in_specs = []
    for g in range(ng):
        in_specs += [col(g * hpg), col(heads + g * hpg), col(2 * heads + g * hpg)]
    tab = pl.BlockSpec((seq, HEAD_DIM), lambda b, h: (0, 0))
    vec = pl.BlockSpec((1, HEAD_DIM), lambda b, h: (0, 0))
    in_specs += [tab, tab, tab, vec, vec]
    out_spec = pl.BlockSpec(blk, lambda b, h: (b, 0, h))
    out_sds = jax.ShapeDtypeStruct((batch, seq, hpg * HEAD_DIM), F32)
    res = pl.pallas_call(
        functools.partial(_attn_kernel, seq=seq, groups=DILATION_GROUPS),
        grid=(batch, hpg),
        in_specs=in_specs,
        out_specs=[out_spec] * (2 * ng),
        out_shape=[out_sds] * (2 * ng),
        scratch_shapes=[pltpu.VMEM((seq, HEAD_DIM), F32)] * (2 + ng),
        compiler_params=_params("parallel", "arbitrary"),
        name="prompt_attention",
    )(*([z] * (3 * ng)), *tables, g_q.reshape(1, HEAD_DIM), g_k.reshape(1, HEAD_DIM))
    return res[:ng], res[ng:]


def _gmlp_kernel(u_ref, vg_ref, lng_ref, lnb_ref, ws_ref, bs_ref, o_ref):
    groups = ws_ref.shape[0]
    ci = lax.broadcasted_iota(jnp.int32, (CHUNK, CHUNK), 0)
    cj = lax.broadcasted_iota(jnp.int32, (CHUNK, CHUNK), 1)
    for g in range(groups):
        cols = slice(g * GMLP_CH, (g + 1) * GMLP_CH)
        ws = jnp.where(cj <= ci, ws_ref[g], 0.0).astype(BF16)
        bias = bs_ref[:, g:g + 1]
        for n in range(u_ref.shape[0] // CHUNK):
            rows = slice(n * CHUNK, (n + 1) * CHUNK)
            v = _gelu(vg_ref[rows, cols])
            mu = jnp.mean(v, axis=-1, keepdims=True)
            vc = v - mu
            var = jnp.mean(vc * vc, axis=-1, keepdims=True)
            vn = vc * lax.rsqrt(var + EPS) * lng_ref[:, cols] + lnb_ref[:, cols]
            mixed = jnp.dot(ws, vn.astype(BF16), preferred_element_type=F32) + bias
            o_ref[rows, cols] = _gelu(u_ref[rows, cols]) * mixed


def _prompt_gmlp(z, ln_g, ln_b, w_s, b_s, *, batch, seq, width, col_block):
    tc = 4 * CHUNK
    groups = w_s.shape[0]
    return pl.pallas_call(
        _gmlp_kernel,
        grid=(batch, seq // tc),
        in_specs=[
            pl.BlockSpec((None, tc, width), lambda b, n: (b, n, col_block)),
            pl.BlockSpec((None, tc, width), lambda b, n: (b, n, col_block + 1)),
            pl.BlockSpec((1, width), lambda b, n: (0, 0)),
            pl.BlockSpec((1, width), lambda b, n: (0, 0)),
            pl.BlockSpec((groups, CHUNK, CHUNK), lambda b, n: (0, 0, 0)),
            pl.BlockSpec((CHUNK, groups), lambda b, n: (0, 0)),
        ],
        out_specs=pl.BlockSpec((None, tc, width), lambda b, n: (b, n, 0)),
        out_shape=jax.ShapeDtypeStruct((batch, seq, width), F32),
        compiler_params=_params("parallel", "parallel"),
        name="prompt_gmlp",
    )(z, z, ln_g.reshape(1, width), ln_b.reshape(1, width), w_s, b_s.T)


def _out_proj_kernel(*refs, n_attn, exact):
    a_refs = refs[:n_attn]
    gm_ref, x_ref, ga_ref, gg_ref, w_ref, o_ref, yn_ref = refs[n_attn:]
    wa = a_refs[0].shape[1]

    @pl.when(pl.program_id(1) == 0)
    def _():
        rows = min(x_ref.shape[0], 128)

        def body(c, carry):
            sl = pl.ds(pl.multiple_of(c * rows, rows), rows)
            parts = [a[sl, :] for a in a_refs]
            ssq = functools.reduce(lambda p, q: p + q,
                                   [jnp.sum(p * p, axis=-1, keepdims=True) for p in parts])
            rs = lax.rsqrt(ssq / (n_attn * wa) + EPS)
            for idx, p in enumerate(parts):
                cs = slice(idx * wa, (idx + 1) * wa)
                yn_ref[sl, cs] = (p * rs * ga_ref[:, cs]).astype(yn_ref.dtype)
            gm = gm_ref[sl, :]
            yn_ref[sl, n_attn * wa:] = (gm * _rms_scale(gm) * gg_ref[...]).astype(yn_ref.dtype)
            return carry

        lax.fori_loop(0, x_ref.shape[0] // rows, body, 0)

    if exact:
        y = _dot3(yn_ref[...], w_ref[...])
    else:
        y = jnp.dot(yn_ref[...], w_ref[...].astype(BF16), preferred_element_type=F32)
    o_ref[...] = x_ref[...] + y


def _out_proj(attn_parts, gm, x, g_attn, g_gmlp, w, *, tm, tn, exact, name):
    m, d = x.shape
    n_attn = len(attn_parts)
    wa = attn_parts[0].shape[1]
    wg = gm.shape[1]
    part = pl.BlockSpec((tm, wa), lambda i, j: (i, 0))
    return pl.pallas_call(
        functools.partial(_out_proj_kernel, n_attn=n_attn, exact=exact),
        grid=(m // tm, d // tn),
        in_specs=[part] * n_attn + [
            pl.BlockSpec((tm, wg), lambda i, j: (i, 0)),
            pl.BlockSpec((tm, tn), lambda i, j: (i, j)),
            pl.BlockSpec((1, n_attn * wa), lambda i, j: (0, 0)),
            pl.BlockSpec((1, wg), lambda i, j: (0, 0)),
            pl.BlockSpec((n_attn * wa + wg, tn), lambda i, j: (0, j)),
        ],
        out_specs=pl.BlockSpec((tm, tn), lambda i, j: (i, j)),
        out_shape=jax.ShapeDtypeStruct((m, d), F32),
        scratch_shapes=[pltpu.VMEM((tm, n_attn * wa + wg), F32 if exact else BF16)],
        compiler_params=_params("parallel", "arbitrary"),
        name=name,
    )(*attn_parts, gm, x, g_attn.reshape(1, -1), g_gmlp.reshape(1, -1), w)


ROUTE_LANES = 128


def _router_kernel(h_ref, g_ref, wr_ref, br_ref, c0_ref, m_ref, route_ref, cnt_ref, carry_ref):
    tm = h_ref.shape[0]
    n_exp = wr_ref.shape[1]

    @pl.when(pl.program_id(0) == 0)
    def _():
        carry_ref[...] = c0_ref[...]

    h = h_ref[...]
    m = h * _rms_scale(h) * g_ref[...]
    m_ref[...] = m
    logits = _dot3(m, wr_ref[...]) + br_ref[...]

    lane = lax.broadcasted_iota(jnp.int32, (tm, n_exp), 1)
    work = logits
    vals, sels, idxs = [], [], []
    for _ in range(TOP_K):
        mx = jnp.max(work, axis=-1, keepdims=True)
        idx = jnp.min(jnp.where(work == mx, lane, n_exp), axis=-1, keepdims=True)
        sel = lane == idx
        vals.append(mx)
        idxs.append(idx)
        sels.append(sel)
        work = jnp.where(sel, -jnp.inf, work)
    es = [jnp.exp(v - vals[0]) for v in vals]
    inv = 1.0 / functools.reduce(lambda a, b: a + b, es)

    hot = functools.reduce(lambda a, b: a + b, [s.astype(F32) for s in sels])
    ri = lax.broadcasted_iota(jnp.int32, (tm, tm), 0)
    rj = lax.broadcasted_iota(jnp.int32, (tm, tm), 1)
    earlier = (rj < ri).astype(BF16)
    base = jnp.dot(earlier, hot.astype(BF16), preferred_element_type=F32) + carry_ref[...]
    ranks = [jnp.sum(jnp.where(s, base, 0.0), axis=-1, keepdims=True) for s in sels]
    carry_ref[...] = carry_ref[...] + jnp.sum(hot, axis=0, keepdims=True)
    cnt_ref[...] = carry_ref[...]

    ol = lax.broadcasted_iota(jnp.int32, (tm, ROUTE_LANES), 1)
    out = jnp.zeros((tm, ROUTE_LANES), F32)
    for k in range(TOP_K):
        out = jnp.where(ol == k, idxs[k].astype(F32), out)
        out = jnp.where(ol == TOP_K + k, ranks[k], out)
        out = jnp.where(ol == 2 * TOP_K + k, es[k] * inv, out)
    route_ref[...] = out


def _router(h, g, w_r, b_r, counts0, *, tm, name):
    t, d = h.shape
    n_exp = w_r.shape[1]
    return pl.pallas_call(
        _router_kernel,
        grid=(t // tm,),
        in_specs=[
            pl.BlockSpec((tm, d), lambda i: (i, 0)),
            pl.BlockSpec((1, d), lambda i: (0, 0)),
            pl.BlockSpec((d, n_exp), lambda i: (0, 0)),
            pl.BlockSpec((1, n_exp), lambda i: (0, 0)),
            pl.BlockSpec((1, n_exp), lambda i: (0, 0)),
        ],
        out_specs=[
            pl.BlockSpec((tm, d), lambda i: (i, 0)),
            pl.BlockSpec((tm, ROUTE_LANES), lambda i: (i, 0)),
            pl.BlockSpec((1, n_exp), lambda i: (0, 0)),
        ],
        out_shape=[
            jax.ShapeDtypeStruct((t, d), F32),
            jax.ShapeDtypeStruct((t, ROUTE_LANES), F32),
            jax.ShapeDtypeStruct((1, n_exp), F32),
        ],
        scratch_shapes=[pltpu.VMEM((1, n_exp), F32)],
        compiler_params=_params("arbitrary"),
        name=name,
    )(h, g.reshape(1, d), w_r, b_r.reshape(1, n_exp), counts0)


def _scatter_kernel(*refs, zero_fill, n_exp):
    if zero_fill:
        pstart_ref, pend_ref, dest_ref, m_hbm, rows_hbm, zeros_ref, sem = refs
    else:
        pstart_ref, pend_ref, dest_ref, m_hbm, _, rows_hbm, zeros_ref, sem = refs
    ts = dest_ref.shape[0] // TOP_K
    i = pl.program_id(0)

    if zero_fill:
        @pl.when(i == 0)
        def _():
            zeros_ref[...] = jnp.zeros_like(zeros_ref)

            def fill(e):
                return pltpu.make_async_copy(
                    zeros_ref, rows_hbm.at[pl.ds(pl.multiple_of(pend_ref[e] - MOE_CHUNK, MOE_CHUNK), MOE_CHUNK)], sem)

            def start(e, carry):
                @pl.when(pend_ref[e] > pstart_ref[e])
                def _():
                    fill(e).start()
                return carry

            def wait(e, carry):
                @pl.when(pend_ref[e] > pstart_ref[e])
                def _():
                    fill(e).wait()
                return carry

            lax.fori_loop(0, n_exp, start, 0)
            lax.fori_loop(0, n_exp, wait, 0)

    def row_copy(src, dst):
        return pltpu.make_async_copy(m_hbm.at[pl.ds(src, 1)], rows_hbm.at[pl.ds(dst, 1)], sem)

    def start(t, carry):
        for k in range(TOP_K):
            row_copy(i * ts + t, dest_ref[TOP_K * t + k]).start()
        return carry

    def wait(t, carry):
        for k in range(TOP_K):
            row_copy(0, 0).wait()
        return carry

    lax.fori_loop(0, ts, start, 0)
    lax.fori_loop(0, ts, wait, 0)


def _scatter_rows(m, dest, pstart, pend, rows_in, *, ts, n_rows, name):
    t, d = m.shape
    zero_fill = rows_in is None
    n_exp = pstart.shape[0]
    in_specs = [
        pl.BlockSpec((ts * TOP_K,), lambda i, *_: (i,), memory_space=pltpu.SMEM),
        pl.BlockSpec(memory_space=pl.ANY),
    ]
    args = [dest, m]
    aliases = {}
    if not zero_fill:
        in_specs.append(pl.BlockSpec(memory_space=pl.ANY))
        args.append(rows_in)
        aliases = {4: 0}
    return pl.pallas_call(
        functools.partial(_scatter_kernel, zero_fill=zero_fill, n_exp=n_exp),
        grid_spec=pltpu.PrefetchScalarGridSpec(
            num_scalar_prefetch=2,
            grid=(t // ts,),
            in_specs=in_specs,
            out_specs=pl.BlockSpec(memory_space=pl.ANY),
            scratch_shapes=[pltpu.VMEM((MOE_CHUNK, d), F32), pltpu.SemaphoreType.DMA(())],
        ),
        out_shape=jax.ShapeDtypeStruct((n_rows, d), F32),
        input_output_aliases=aliases,
        compiler_params=_params("arbitrary"),
        name=name,
    )(pstart, pend, *args)


def _expert_kernel(ie_ref, ir_ref, inch_ref, rows_hbm, wg_ref, wl_ref, wd_ref, bg_ref, bl_ref,
                   bd_ref, ys_hbm, stage, xb, acc, wgb, wlb, wdb, sem_in, sem_out, *, n_ff):
    i = pl.program_id(0)
    f = pl.program_id(1)
    nch = inch_ref[i]
    row0 = pl.multiple_of(ir_ref[i], MOE_CHUNK)

    def chunk(c):
        return slice(c * MOE_CHUNK, (c + 1) * MOE_CHUNK)

    def load(c):
        return pltpu.make_async_copy(
            rows_hbm.at[pl.ds(row0 + c * MOE_CHUNK, MOE_CHUNK)], stage.at[chunk(c)], sem_in.at[c])

    def store(c):
        return pltpu.make_async_copy(
            acc.at[chunk(c)], ys_hbm.at[pl.ds(row0 + c * MOE_CHUNK, MOE_CHUNK)], sem_out.at[c])

    def for_chunks(fn):
        for c in range(MOE_MAX_CHUNKS):
            pl.when(c < nch)(functools.partial(fn, c))

    @pl.when(nch > 0)
    def _():
        @pl.when(f == 0)
        def _():
            for_chunks(lambda c: load(c).start())

            def land(c):
                load(c).wait()
                xb[chunk(c), :] = stage[chunk(c), :].astype(BF16)

            for_chunks(land)

        wgb[...] = wg_ref[...].astype(BF16)
        wlb[...] = wl_ref[...].astype(BF16)
        wdb[...] = wd_ref[...].astype(BF16)

        def compute(c):
            x = xb[chunk(c), :]
            gate = jnp.dot(x, wgb[...], preferred_element_type=F32) + bg_ref[...]
            lin = jnp.dot(x, wlb[...], preferred_element_type=F32) + bl_ref[...]
            gate = jnp.minimum(gate, SWIGLU_LIMIT)
            lin = jnp.clip(lin, -SWIGLU_LIMIT, SWIGLU_LIMIT)
            hdn = gate * jax.nn.sigmoid(SWIGLU_ALPHA * gate) * (lin + 1.0)
            y = jnp.dot(hdn.astype(BF16), wdb[...], preferred_element_type=F32)

            @pl.when(f == 0)
            def _():
                acc[chunk(c), :] = y + bd_ref[...]

            @pl.when(f > 0)
            def _():
                acc[chunk(c), :] += y

        for_chunks(compute)

        @pl.when(f == n_ff - 1)
        def _():
            for_chunks(lambda c: store(c).start())
            for_chunks(lambda c: store(c).wait())


def _expert_ffn(rows, w_gu, b_gu, w_down, b_down, item_e, item_row0, item_nch):
    n_rows, d = rows.shape
    n_exp, _, two_ff = w_gu.shape
    d_ff = two_ff // 2
    tf = MOE_FF_TILE
    n_ff = d_ff // tf
    n_items = item_e.shape[0]
    r_max = MOE_MAX_CHUNKS * MOE_CHUNK

    def fidx(i, f, inch):
        return jnp.where(inch[i] > 0, f, n_ff - 1)

    return pl.pallas_call(
        functools.partial(_expert_kernel, n_ff=n_ff),
        grid_spec=pltpu.PrefetchScalarGridSpec(
            num_scalar_prefetch=3,
            grid=(n_items, n_ff),
            in_specs=[
                pl.BlockSpec(memory_space=pl.ANY),
                pl.BlockSpec((None, d, tf), lambda i, f, ie, ir, inch: (ie[i], 0, fidx(i, f, inch))),
                pl.BlockSpec((None, d, tf),
                             lambda i, f, ie, ir, inch: (ie[i], 0, n_ff + fidx(i, f, inch))),
                pl.BlockSpec((None, tf, d), lambda i, f, ie, ir, inch: (ie[i], fidx(i, f, inch), 0)),
                pl.BlockSpec((None, 1, tf), lambda i, f, ie, ir, inch: (ie[i], 0, fidx(i, f, inch))),
                pl.BlockSpec((None, 1, tf),
                             lambda i, f, ie, ir, inch: (ie[i], 0, n_ff + fidx(i, f, inch))),
                pl.BlockSpec((None, 1, d), lambda i, f, ie, ir, inch: (ie[i], 0, 0)),
            ],
            out_specs=pl.BlockSpec(memory_space=pl.ANY),
            scratch_shapes=[
                pltpu.VMEM((r_max, d), F32),
                pltpu.VMEM((r_max, d), BF16),
                pltpu.VMEM((r_max, d), F32),
                pltpu.VMEM((d, tf), BF16),
                pltpu.VMEM((d, tf), BF16),
                pltpu.VMEM((tf, d), BF16),
                pltpu.SemaphoreType.DMA((MOE_MAX_CHUNKS,)),
                pltpu.SemaphoreType.DMA((MOE_MAX_CHUNKS,)),
            ],
        ),
        out_shape=jax.ShapeDtypeStruct((n_rows, d), F32),
        compiler_params=_params("arbitrary", "arbitrary"),
        name="expert_ffn",
    )(item_e, item_row0, item_nch, rows, w_gu, w_gu, w_down,
      b_gu.reshape(n_exp, 1, two_ff), b_gu.reshape(n_exp, 1, two_ff),
      b_down.reshape(n_exp, 1, d))


def _combine_kernel(dest_ref, route_ref, h_ref, ys_hbm, o_ref, buf, sem):
    tg = h_ref.shape[0]

    def row_copy(src, k, t):
        return pltpu.make_async_copy(ys_hbm.at[pl.ds(src, 1)], buf.at[k, pl.ds(t, 1)], sem)

    def start(t, carry):
        for k in range(TOP_K):
            row_copy(dest_ref[TOP_K * t + k], k, t).start()
        return carry

    def wait(t, carry):
        for k in range(TOP_K):
            row_copy(0, k, 0).wait()
        return carry

    lax.fori_loop(0, tg, start, 0)
    lax.fori_loop(0, tg, wait, 0)
    out = h_ref[...]
    for k in range(TOP_K):
        out = out + route_ref[:, 2 * TOP_K + k:2 * TOP_K + k + 1] * buf[k]
    o_ref[...] = out


def _combine(dest, route, h, ys, *, tg, name):
    t, d = h.shape
    return pl.pallas_call(
        _combine_kernel,
        grid=(t // tg,),
        in_specs=[
            pl.BlockSpec((tg * TOP_K,), lambda i: (i,), memory_space=pltpu.SMEM),
            pl.BlockSpec((tg, ROUTE_LANES), lambda i: (i, 0)),
            pl.BlockSpec((tg, d), lambda i: (i, 0)),
            pl.BlockSpec(memory_space=pl.ANY),
        ],
        out_specs=pl.BlockSpec((tg, d), lambda i: (i, 0)),
        out_shape=jax.ShapeDtypeStruct((t, d), F32),
        scratch_shapes=[pltpu.VMEM((TOP_K, tg, d), F32), pltpu.SemaphoreType.DMA(())],
        compiler_params=_params("arbitrary"),
        name=name,
    )(dest, route, h, ys)


def _ple_kernel(h_ref, hc_ref, p_ref, g_ref, wg_ref, wp_ref, o_ref, hn_ref):
    @pl.when(pl.program_id(1) == 0)
    def _():
        rows = min(h_ref.shape[0], 128)

        def body(c, carry):
            sl = pl.ds(pl.multiple_of(c * rows, rows), rows)
            h = h_ref[sl, :]
            hn_ref[sl, :] = (h * _rms_scale(h) * g_ref[...]).astype(BF16)
            return carry

        lax.fori_loop(0, h_ref.shape[0] // rows, body, 0)

    gate = jax.nn.sigmoid(jnp.dot(hn_ref[...], wg_ref[...].astype(BF16), preferred_element_type=F32))
    proj = jnp.dot(p_ref[...].astype(BF16), wp_ref[...].astype(BF16), preferred_element_type=F32)
    o_ref[...] = hc_ref[...] + gate * proj


def _ple(h, p, g, w_gate, w_proj, *, tm, tn, name):
    t, d = h.shape
    pd = p.shape[1]
    return pl.pallas_call(
        _ple_kernel,
        grid=(t // tm, d // tn),
        in_specs=[
            pl.BlockSpec((tm, d), lambda i, j: (i, 0)),
            pl.BlockSpec((tm, tn), lambda i, j: (i, j)),
            pl.BlockSpec((tm, pd), lambda i, j: (i, 0)),
            pl.BlockSpec((1, d), lambda i, j: (0, 0)),
            pl.BlockSpec((d, tn), lambda i, j: (0, j)),
            pl.BlockSpec((pd, tn), lambda i, j: (0, j)),
        ],
        out_specs=pl.BlockSpec((tm, tn), lambda i, j: (i, j)),
        out_shape=jax.ShapeDtypeStruct((t, d), F32),
        scratch_shapes=[pltpu.VMEM((tm, d), BF16)],
        compiler_params=_params("parallel", "arbitrary"),
        name=name,
    )(h, h, p, g.reshape(1, d), w_gate, w_proj)


def _decode_kernel(*refs, groups, hpg):
    ng = len(groups)
    z_ref = refs[0]
    caches = refs[1:1 + ng]
    (cos_ref, sa_ref, sb_ref, gq_ref, gk_ref, lng_ref, lnb_ref, ws0_ref, bs0_ref,
     attn_ref, kn_ref, gm_ref, vn_ref) = refs[1 + ng:]
    heads = ng * hpg
    aw = heads * HEAD_DIM
    scale = HEAD_DIM ** -0.5
    tabs = (cos_ref[...], sa_ref[...], sb_ref[...])

    def head(base, hh):
        return z_ref[:, base + hh * HEAD_DIM:base + (hh + 1) * HEAD_DIM]

    outs, lses = [], []
    for g in range(ng):
        c_ref = caches[g]
        for j in range(hpg):
            hh = g * hpg + j
            qn = _norm_rope(head(0, hh), gq_ref[...], *tabs) * scale
            kn = _norm_rope(head(aw, hh), gk_ref[...], *tabs)
            vv = head(2 * aw, hh)
            kn_ref[:, hh * HEAD_DIM:(hh + 1) * HEAD_DIM] = kn
            kc = c_ref[:, j * HEAD_DIM:(j + 1) * HEAD_DIM]
            vc = c_ref[:, (hpg + j) * HEAD_DIM:(hpg + j + 1) * HEAD_DIM]
            s_c = jnp.sum(kc * qn, axis=-1, keepdims=True)
            s_n = jnp.sum(kn * qn, axis=-1, keepdims=True)
            m = jnp.maximum(jnp.max(s_c, axis=0, keepdims=True), s_n)
            p_c = jnp.exp(s_c - m)
            p_n = jnp.exp(s_n - m)
            den = jnp.sum(p_c, axis=0, keepdims=True) + p_n
            o = (jnp.sum(p_c * vc, axis=0, keepdims=True) + p_n * vv) / den
            outs.append(o)
            lses.append(m + jnp.log(den))

    for j in range(hpg):
        ls = [lses[g * hpg + j] for g in range(ng)]
        mx = functools.reduce(jnp.maximum, ls)
        es = [jnp.exp(l - mx) for l in ls]
        inv = 1.0 / functools.reduce(lambda a, b: a + b, es)
        for g in range(ng):
            hh = g * hpg + j
            attn_ref[:, hh * HEAD_DIM:(hh + 1) * HEAD_DIM] = outs[hh] * (es[g] * inv)

    gw = gm_ref.shape[1]
    for g in range(gw // GMLP_CH):
        cols = slice(g * GMLP_CH, (g + 1) * GMLP_CH)
        v = _gelu(z_ref[:, 3 * aw + gw + g * GMLP_CH:3 * aw + gw + (g + 1) * GMLP_CH])
        mu = jnp.mean(v, axis=-1, keepdims=True)
        vc = v - mu
        var = jnp.mean(vc * vc, axis=-1, keepdims=True)
        vn = vc * lax.rsqrt(var + EPS) * lng_ref[:, cols] + lnb_ref[:, cols]
        vn_ref[:, cols] = vn
        u = _gelu(z_ref[:, 3 * aw + g * GMLP_CH:3 * aw + (g + 1) * GMLP_CH])
        gm_ref[:, cols] = u * (ws0_ref[:, cols] * vn + bs0_ref[:, cols])


def _sample_mixers(zs, caches, tables, g_q, g_k, ln_g, ln_b, w_s, b_s, *, hpg):
    bsz, in_width = zs.shape
    ng = len(DILATION_GROUPS)
    aw = ng * hpg * HEAD_DIM
    gw = (in_width - 3 * aw) // 2
    row = 2 * hpg * HEAD_DIM
    in_specs = [pl.BlockSpec((None, 1, in_width), lambda b: (b, 0, 0))]
    cache_args = []
    for (window, dil), c in zip(DILATION_GROUPS, caches):
        wc = c.shape[1]
        assert wc == window and c.shape[0] == bsz, "window caches must hold a full window"
        steps = wc // dil
        cache_args.append(c.reshape(bsz, steps, dil * row))
        in_specs.append(pl.BlockSpec((None, steps, row), lambda b: (b, 0, 0)))
    vec = pl.BlockSpec((1, HEAD_DIM), lambda b: (0, 0))
    wide = pl.BlockSpec((1, gw), lambda b: (0, 0))
    in_specs += [vec] * 5 + [wide] * 4

    def out(w):
        return pl.BlockSpec((None, 1, w), lambda b: (b, 0, 0)), jax.ShapeDtypeStruct((bsz, 1, w), F32)

    specs, shapes = zip(out(aw), out(aw), out(gw), out(gw))
    groups = w_s.shape[0]
    ws0 = jnp.repeat(w_s[:, 0, 0], GMLP_CH).reshape(1, gw)
    bs0 = jnp.repeat(b_s[:, 0], GMLP_CH).reshape(1, gw)
    assert groups * GMLP_CH == gw
    res = pl.pallas_call(
        functools.partial(_decode_kernel, groups=DILATION_GROUPS, hpg=hpg),
        grid=(bsz,),
        in_specs=in_specs,
        out_specs=list(specs),
        out_shape=list(shapes),
        compiler_params=_params("parallel"),
        name="sample_mixers",
    )(zs.reshape(bsz, 1, in_width), *cache_args, *tables,
      g_q.reshape(1, HEAD_DIM), g_k.reshape(1, HEAD_DIM),
      ln_g.reshape(1, gw), ln_b.reshape(1, gw), ws0, bs0)
    return [r.reshape(bsz, -1) for r in res]


def _expert_layout(counts, n_items):
    counts = counts.astype(jnp.int32)
    padded = (counts + MOE_CHUNK - 1) // MOE_CHUNK * MOE_CHUNK
    pend = jnp.cumsum(padded)
    pstart = pend - padded
    r_max = MOE_MAX_CHUNKS * MOE_CHUNK
    per_e = (padded + r_max - 1) // r_max
    item_end = jnp.cumsum(per_e)
    total = item_end[-1]
    ids = jnp.arange(n_items, dtype=jnp.int32)
    clamped = jnp.minimum(ids, total - 1)
    e = jnp.searchsorted(item_end, clamped, side="right").astype(jnp.int32)
    local = clamped - (item_end[e] - per_e[e])
    row0 = pstart[e] + local * r_max
    nch = jnp.minimum(padded[e] - local * r_max, r_max) // MOE_CHUNK
    nch = jnp.where(ids < total, nch, 0)
    return pstart, pend, e, row0.astype(jnp.int32), nch.astype(jnp.int32)


def _destinations(route, pstart):
    e = route[:, :TOP_K].astype(jnp.int32)
    rank = route[:, TOP_K:2 * TOP_K].astype(jnp.int32)
    return (pstart[e] + rank).reshape(-1)


def kernel(x_prompt, x_sample, cache_kv_w128, cache_kv_w512, cache_kv_w2048, p_prompt, p_sample,
           g_mix, w_in, g_q, g_k, gmlp_ln_g, gmlp_ln_b, gmlp_w_s, gmlp_b_s, g_out_attn, g_out_gmlp,
           w_out, g_ffn, w_router, b_router, w_gu, b_gu, w_down, b_down, g_ple, w_ple_gate,
           w_ple_proj):
    depth = g_mix.shape[0]
    assert depth == 1, "single-layer step"
    batch, seq, d = x_prompt.shape
    dec_batch, dec_seq, _ = x_sample.shape
    assert dec_seq == 1
    in_width = w_in.shape[2]
    gw = gmlp_ln_g.shape[1] * gmlp_ln_g.shape[2]
    aw = (in_width - 2 * gw) // 3
    ng = len(DILATION_GROUPS)
    hpg = aw // HEAD_DIM // ng
    n_exp = w_router.shape[2]
    caches = (cache_kv_w128[0], cache_kv_w512[0], cache_kv_w2048[0])
    tp = batch * seq
    ts = dec_batch * dec_seq

    xp = x_prompt.reshape(tp, d)
    z = _norm_matmul(xp, g_mix[0], w_in[0], tm=1024, tn=512, exact=False, name="prompt_in_proj")
    z3 = z.reshape(batch, seq, in_width)
    tables_p = _rope_tables(jnp.arange(seq, dtype=jnp.int32))
    attn_p, kn_p = _prompt_attention(z3, g_q[0], g_k[0], tables_p, batch=batch, seq=seq, hpg=hpg)
    gm_p = _prompt_gmlp(z3, gmlp_ln_g[0], gmlp_ln_b[0], gmlp_w_s[0], gmlp_b_s[0],
                        batch=batch, seq=seq, width=gw, col_block=3 * aw // gw)
    hp = _out_proj([a.reshape(tp, -1) for a in attn_p], gm_p.reshape(tp, gw), xp,
                   g_out_attn[0], g_out_gmlp[0], w_out[0], tm=1024, tn=512, exact=False,
                   name="prompt_out_proj")

    xs = x_sample.reshape(ts, d)
    zs = _norm_matmul(xs, g_mix[0], w_in[0], tm=ts, tn=512, exact=True, name="sample_in_proj")
    tables_s = _rope_tables(PAST_LEN + jnp.arange(dec_seq, dtype=jnp.int32))
    attn_s, kn_s, gm_s, vn_s = _sample_mixers(
        zs, [c.reshape(c.shape[0], c.shape[1], -1) for c in caches], tables_s, g_q[0], g_k[0],
        gmlp_ln_g[0], gmlp_ln_b[0], gmlp_w_s[0], gmlp_b_s[0], hpg=hpg)
    gwid = hpg * HEAD_DIM
    hs = _out_proj([attn_s[:, g * gwid:(g + 1) * gwid] for g in range(ng)], gm_s, xs,
                   g_out_attn[0], g_out_gmlp[0], w_out[0], tm=ts, tn=512, exact=True,
                   name="sample_out_proj")

    zero_counts = jnp.zeros((1, n_exp), F32)
    m_p, route_p, counts_p = _router(hp, g_ffn[0], w_router[0], b_router[0], zero_counts,
                                     tm=512, name="prompt_router")
    m_s, route_s, counts = _router(hs, g_ffn[0], w_router[0], b_router[0], counts_p,
                                   tm=ts, name="sample_router")
    n_assign = (tp + ts) * TOP_K
    n_chunks = -(-n_assign // MOE_CHUNK) + n_exp
    n_rows = n_chunks * MOE_CHUNK
    n_items = n_exp + -(-n_chunks // MOE_MAX_CHUNKS)
    pstart, pend, item_e, item_row0, item_nch = _expert_layout(counts[0], n_items)
    dest_p = _destinations(route_p, pstart)
    dest_s = _destinations(route_s, pstart)
    rows = _scatter_rows(m_p, dest_p, pstart, pend, None, ts=512, n_rows=n_rows,
                         name="prompt_scatter")
    rows = _scatter_rows(m_s, dest_s, pstart, pend, rows, ts=ts, n_rows=n_rows,
                         name="sample_scatter")
    ys = _expert_ffn(rows, w_gu[0], b_gu[0], w_down[0], b_down[0], item_e, item_row0, item_nch)
    hp = _combine(dest_p, route_p, hp, ys, tg=256, name="prompt_combine")
    hs = _combine(dest_s, route_s, hs, ys, tg=ts, name="sample_combine")

    yp = _ple(hp, p_prompt[0].reshape(tp, -1), g_ple[0], w_ple_gate[0], w_ple_proj[0],
              tm=1024, tn=512, name="prompt_ple")
    ysm = _ple(hs, p_sample[0].reshape(ts, -1), g_ple[0], w_ple_gate[0], w_ple_proj[0],
               tm=ts, tn=512, name="sample_ple")

    v0 = 2 * aw
    kv_prompt, kv_sample = [], []
    for g, (window, _) in enumerate(DILATION_GROUPS):
        wc = min(window, seq)
        kg = kn_p[g][:, seq - wc:].reshape(batch, wc, hpg, HEAD_DIM)
        vg = z3[:, seq - wc:, v0 + g * gwid:v0 + (g + 1) * gwid].reshape(batch, wc, hpg, HEAD_DIM)
        kv_prompt.append(jnp.stack([kg, vg], axis=2)[None])
        ksg = kn_s[:, g * gwid:(g + 1) * gwid].reshape(dec_batch, dec_seq, hpg, HEAD_DIM)
        vsg = zs[:, v0 + g * gwid:v0 + (g + 1) * gwid].reshape(dec_batch, dec_seq, hpg, HEAD_DIM)
        kv_sample.append(jnp.stack([ksg, vsg], axis=2)[None])
    gmlp_v = vn_s.reshape(1, dec_batch, dec_seq, gw // GMLP_CH, GMLP_CH)
    return (yp.reshape(batch, seq, d), ysm.reshape(dec_batch, dec_seq, d),
            *kv_prompt, *kv_sample, gmlp_v)
```

```python
import functools

import jax
import jax.numpy as jnp
from jax import lax
from jax.experimental import pallas as pl
from jax.experimental.pallas import tpu as pltpu

F32 = jnp.float32
BF16 = jnp.bfloat16

EPS = 1e-6
HEAD_DIM = 128
ROT_DIM = HEAD_DIM // 4
ROPE_THETA = 500000.0
DILATION_GROUPS = ((128, 1), (512, 4), (2048, 16))
PAST_LEN = 16384
QBLK = 128
CHUNK = 128
GMLP_CH = 128
TOP_K = 4
SWIGLU_LIMIT = 7.0
SWIGLU_ALPHA = 1.702

V7X_VMEM_BYTES = 64 * 1024 * 1024
VMEM_LIMIT = V7X_VMEM_BYTES - 8 * 1024 * 1024

MOE_CHUNK = 256
MOE_MAX_CHUNKS = 5
MOE_FF_TILE = 256


def _params(*sem):
    return pltpu.CompilerParams(dimension_semantics=sem, vmem_limit_bytes=VMEM_LIMIT)


def _split_bf16(a):
    hi = a.astype(BF16)
    lo = (a - hi.astype(F32)).astype(BF16)
    return hi, lo


def _mm(a, b, exact=False):
    d = functools.partial(jnp.dot, preferred_element_type=F32)
    if not exact:
        return d(a.astype(BF16), b.astype(BF16))
    ah, al = _split_bf16(a)
    bh, bl = _split_bf16(b)
    return d(ah, bh) + (d(ah, bl) + d(al, bh))


def _gelu(x):
    return 0.5 * x * (1.0 + jnp.tanh(0.7978845608028654 * (x + 0.044715 * (x * x * x))))


def _rms_scale(x):
    return lax.rsqrt(jnp.mean(x * x, axis=-1, keepdims=True) + EPS)


def _norm_matmul_kernel(x_ref, g_ref, w_ref, o_ref, xn_ref, *, exact):
    @pl.when(pl.program_id(1) == 0)
    def _():
        rows = min(x_ref.shape[0], 128)

        def body(c, carry):
            sl = pl.ds(pl.multiple_of(c * rows, rows), rows)
            x = x_ref[sl, :]
            xn_ref[sl, :] = (x * _rms_scale(x) * g_ref[...]).astype(xn_ref.dtype)
            return carry

        lax.fori_loop(0, x_ref.shape[0] // rows, body, 0)

    o_ref[...] = _mm(xn_ref[...], w_ref[...], exact)


def _norm_matmul(x, g, w, *, tm, tn, name, exact=False):
    m, k = x.shape
    n = w.shape[1]
    return pl.pallas_call(
        functools.partial(_norm_matmul_kernel, exact=exact),
        grid=(m // tm, n // tn),
        in_specs=[
            pl.BlockSpec((tm, k), lambda i, j: (i, 0)),
            pl.BlockSpec((1, k), lambda i, j: (0, 0)),
            pl.BlockSpec((k, tn), lambda i, j: (0, j)),
        ],
        out_specs=pl.BlockSpec((tm, tn), lambda i, j: (i, j)),
        out_shape=jax.ShapeDtypeStruct((m, n), F32),
        scratch_shapes=[pltpu.VMEM((tm, k), F32 if exact else BF16)],
        compiler_params=_params("parallel", "arbitrary"),
        name=name,
    )(x, g.reshape(1, k), w)


def _rope_tables(pos):
    half = ROT_DIM // 2
    inv = 1.0 / (ROPE_THETA ** (jnp.arange(half, dtype=F32) * (2.0 / ROT_DIM)))
    ang = pos.astype(F32)[:, None] * inv[None, :]
    cos, sin = jnp.cos(ang), jnp.sin(ang)
    t = pos.shape[0]
    pad = HEAD_DIM - ROT_DIM
    cos_t = jnp.concatenate([cos, cos, jnp.ones((t, pad), F32)], axis=1)
    sa = jnp.concatenate([-sin, jnp.zeros((t, HEAD_DIM - half), F32)], axis=1)
    sb = jnp.concatenate([jnp.zeros((t, half), F32), sin, jnp.zeros((t, pad), F32)], axis=1)
    return cos_t, sa, sb


def _norm_rope(x, gain, cos_t, sa, sb):
    half = ROT_DIM // 2
    xn = x * _rms_scale(x) * gain
    up = pltpu.roll(xn, HEAD_DIM - half, 1)
    dn = pltpu.roll(xn, half, 1)
    return xn * cos_t + up * sa + dn * sb


def _band_block(q, k, v, mask):
    s = lax.dot_general(q.astype(BF16), k.astype(BF16), (((1,), (1,)), ((), ())),
                        preferred_element_type=F32)
    s = jnp.where(mask, s, -jnp.inf)
    m = jnp.max(s, axis=-1, keepdims=True)
    p = jnp.exp(s - m)
    den = jnp.sum(p, axis=-1, keepdims=True)
    o = jnp.dot((p / den).astype(BF16), v.astype(BF16), preferred_element_type=F32)
    return o, m + jnp.log(den)


def _attn_kernel(*refs, seq, groups):
    ng = len(groups)
    qkv = refs[:3 * ng]
    cos_ref, sa_ref, sb_ref, gq_ref, gk_ref = refs[3 * ng:3 * ng + 5]
    outs = refs[3 * ng + 5:3 * ng + 5 + 2 * ng]
    o_refs, kn_refs = outs[:ng], outs[ng:]
    qs, ks = refs[3 * ng + 5 + 2 * ng:3 * ng + 7 + 2 * ng]
    ls_refs = refs[3 * ng + 7 + 2 * ng:]
    scale = HEAD_DIM ** -0.5
    nrm_rows = 256

    qi = lax.broadcasted_iota(jnp.int32, (QBLK, 2 * QBLK), 0)
    kj = lax.broadcasted_iota(jnp.int32, (QBLK, 2 * QBLK), 1)
    qi1 = lax.broadcasted_iota(jnp.int32, (QBLK, QBLK), 0)
    kj1 = lax.broadcasted_iota(jnp.int32, (QBLK, QBLK), 1)

    for g, (window, dil) in enumerate(groups):
        q_ref, k_ref, v_ref = qkv[3 * g:3 * g + 3]
        o_ref, kn_ref, ls_ref = o_refs[g], kn_refs[g], ls_refs[g]
        n_steps = window // dil
        nb = seq // dil // QBLK

        def prep(c, carry, q_ref=q_ref, k_ref=k_ref, kn_ref=kn_ref):
            sl = pl.ds(pl.multiple_of(c * nrm_rows, nrm_rows), nrm_rows)
            tabs = (cos_ref[sl, :], sa_ref[sl, :], sb_ref[sl, :])
            qs[sl, :] = _norm_rope(q_ref[sl, :], gq_ref[...], *tabs) * scale
            kn = _norm_rope(k_ref[sl, :], gk_ref[...], *tabs)
            kn_ref[sl, :] = kn
            ks[sl, :] = kn
            return carry

        lax.fori_loop(0, seq // nrm_rows, prep, 0)

        def rows(start, size, dil=dil):
            return pl.ds(start, size) if dil == 1 else pl.ds(start, size, stride=dil)

        band_mask = (kj >= qi + (QBLK - n_steps)) & (kj <= qi + QBLK)
        first_mask = (kj1 <= qi1) & (qi1 - kj1 <= n_steps)

        def one_class(r, carry, v_ref=v_ref, o_ref=o_ref, ls_ref=ls_ref, rows=rows,
                      band_mask=band_mask, first_mask=first_mask, dil=dil, nb=nb):
            sl0 = rows(r, QBLK)
            o, lse = _band_block(qs[sl0, :], ks[sl0, :], v_ref[sl0, :], first_mask)
            o_ref[sl0, :] = o
            ls_ref[sl0, :] = jnp.broadcast_to(lse, (QBLK, HEAD_DIM))

            def one_block(lb, carry2):
                q0 = r + dil * QBLK * lb
                k0 = q0 - dil * QBLK
                slq, slk = rows(q0, QBLK), rows(k0, 2 * QBLK)
                o, lse = _band_block(qs[slq, :], ks[slk, :], v_ref[slk, :], band_mask)
                o_ref[slq, :] = o
                ls_ref[slq, :] = jnp.broadcast_to(lse, (QBLK, HEAD_DIM))
                return carry2

            lax.fori_loop(1, nb, one_block, 0, unroll=min(3, max(nb - 1, 1)))
            return carry

        lax.fori_loop(0, dil, one_class, 0, unroll=min(4, dil) if nb == 1 else 1)

    def merge(c, carry):
        sl = pl.ds(pl.multiple_of(c * nrm_rows, nrm_rows), nrm_rows)
        ls = [ls_refs[g][sl, :] for g in range(ng)]
        mx = functools.reduce(jnp.maximum, ls)
        es = [jnp.exp(l - mx) for l in ls]
        inv = 1.0 / functools.reduce(lambda a, b: a + b, es)
        for g in range(ng):
            o_refs[g][sl, :] = o_refs[g][sl, :] * (es[g] * inv)
        return carry

    lax.fori_loop(0, seq // nrm_rows, merge, 0)


def _prompt_attention(z, g_q, g_k, tables, *, batch, seq, hpg):
    ng = len(DILATION_GROUPS)
    heads = ng * hpg
    blk = (None, seq, HEAD_DIM)

    def col(c):
        return pl.BlockSpec(blk, lambda b, h, c=c: (b, 0, c + h))

    in_specs = []
    for g in range(ng):
        in_specs += [col(g * hpg), col(heads + g * hpg), col(2 * heads + g * hpg)]
    tab = pl.BlockSpec((seq, HEAD_DIM), lambda b, h: (0, 0))
    vec = pl.BlockSpec((1, HEAD_DIM), lambda b, h: (0, 0))
    in_specs += [tab, tab, tab, vec, vec]
    out_spec = pl.BlockSpec(blk, lambda b, h: (b, 0, h))
    out_sds = jax.ShapeDtypeStruct((batch, seq, hpg * HEAD_DIM), F32)
    res = pl.pallas_call(
        functools.partial(_attn_kernel, seq=seq, groups=DILATION_GROUPS),
        grid=(batch, hpg),
        in_specs=in_specs,
        out_specs=[out_spec] * (2 * ng),
        out_shape=[out_sds] * (2 * ng),
        scratch_shapes=[pltpu.VMEM((seq, HEAD_DIM), F32)] * (2 + ng),
        compiler_params=_params("parallel", "arbitrary"),
        name="prompt_attention",
    )(*([z] * (3 * ng)), *tables, g_q.reshape(1, HEAD_DIM), g_k.reshape(1, HEAD_DIM))
    return res[:ng], res[ng:]


def _gmlp_kernel(u_ref, vg_ref, lng_ref, lnb_ref, ws_ref, bs_ref, o_ref):
    groups = ws_ref.shape[0]
    ci = lax.broadcasted_iota(jnp.int32, (CHUNK, CHUNK), 0)
    cj = lax.broadcasted_iota(jnp.int32, (CHUNK, CHUNK), 1)
    for g in range(groups):
        cols = slice(g * GMLP_CH, (g + 1) * GMLP_CH)
        ws = jnp.where(cj <= ci, ws_ref[g], 0.0).astype(BF16)
        bias = bs_ref[:, g:g + 1]
        for n in range(u_ref.shape[0] // CHUNK):
            rows = slice(n * CHUNK, (n + 1) * CHUNK)
            v = _gelu(vg_ref[rows, cols])
            mu = jnp.mean(v, axis=-1, keepdims=True)
            vc = v - mu
            var = jnp.mean(vc * vc, axis=-1, keepdims=True)
            vn = vc * lax.rsqrt(var + EPS) * lng_ref[:, cols] + lnb_ref[:, cols]
            mixed = jnp.dot(ws, vn.astype(BF16), preferred_element_type=F32) + bias
            o_ref[rows, cols] = _gelu(u_ref[rows, cols]) * mixed


def _prompt_gmlp(z, ln_g, ln_b, w_s, b_s, *, batch, seq, width, col_block):
    tc = 4 * CHUNK
    groups = w_s.shape[0]
    return pl.pallas_call(
        _gmlp_kernel,
        grid=(batch, seq // tc),
        in_specs=[
            pl.BlockSpec((None, tc, width), lambda b, n: (b, n, col_block)),
            pl.BlockSpec((None, tc, width), lambda b, n: (b, n, col_block + 1)),
            pl.BlockSpec((1, width), lambda b, n: (0, 0)),
            pl.BlockSpec((1, width), lambda b, n: (0, 0)),
            pl.BlockSpec((groups, CHUNK, CHUNK), lambda b, n: (0, 0, 0)),
            pl.BlockSpec((CHUNK, groups), lambda b, n: (0, 0)),
        ],
        out_specs=pl.BlockSpec((None, tc, width), lambda b, n: (b, n, 0)),
        out_shape=jax.ShapeDtypeStruct((batch, seq, width), F32),
        compiler_params=_params("parallel", "parallel"),
        name="prompt_gmlp",
    )(z, z, ln_g.reshape(1, width), ln_b.reshape(1, width), w_s, b_s.T)


def _out_proj_kernel(*refs, n_attn, exact):
    a_refs = refs[:n_attn]
    gm_ref, x_ref, ga_ref, gg_ref, w_ref, o_ref, yn_ref = refs[n_attn:]
    wa = a_refs[0].shape[1]

    @pl.when(pl.program_id(1) == 0)
    def _():
        rows = min(x_ref.shape[0], 128)

        def body(c, carry):
            sl = pl.ds(pl.multiple_of(c * rows, rows), rows)
            parts = [a[sl, :] for a in a_refs]
            ssq = functools.reduce(lambda p, q: p + q,
                                   [jnp.sum(p * p, axis=-1, keepdims=True) for p in parts])
            rs = lax.rsqrt(ssq / (n_attn * wa) + EPS)
            for idx, p in enumerate(parts):
                cs = slice(idx * wa, (idx + 1) * wa)
                yn_ref[sl, cs] = (p * rs * ga_ref[:, cs]).astype(yn_ref.dtype)
            gm = gm_ref[sl, :]
            yn_ref[sl, n_attn * wa:] = (gm * _rms_scale(gm) * gg_ref[...]).astype(yn_ref.dtype)
            return carry

        lax.fori_loop(0, x_ref.shape[0] // rows, body, 0)

    o_ref[...] = x_ref[...] + _mm(yn_ref[...], w_ref[...], exact)


def _out_proj(attn_parts, gm, x, g_attn, g_gmlp, w, *, tm, tn, name, exact=False):
    m, d = x.shape
    n_attn = len(attn_parts)
    wa = attn_parts[0].shape[1]
    wg = gm.shape[1]
    part = pl.BlockSpec((tm, wa), lambda i, j: (i, 0))
    return pl.pallas_call(
        functools.partial(_out_proj_kernel, n_attn=n_attn, exact=exact),
        grid=(m // tm, d // tn),
        in_specs=[part] * n_attn + [
            pl.BlockSpec((tm, wg), lambda i, j: (i, 0)),
            pl.BlockSpec((tm, tn), lambda i, j: (i, j)),
            pl.BlockSpec((1, n_attn * wa), lambda i, j: (0, 0)),
            pl.BlockSpec((1, wg), lambda i, j: (0, 0)),
            pl.BlockSpec((n_attn * wa + wg, tn), lambda i, j: (0, j)),
        ],
        out_specs=pl.BlockSpec((tm, tn), lambda i, j: (i, j)),
        out_shape=jax.ShapeDtypeStruct((m, d), F32),
        scratch_shapes=[pltpu.VMEM((tm, n_attn * wa + wg), F32 if exact else BF16)],
        compiler_params=_params("parallel", "arbitrary"),
        name=name,
    )(*attn_parts, gm, x, g_attn.reshape(1, -1), g_gmlp.reshape(1, -1), w)


ROUTE_LANES = 128


def _router_kernel(h_ref, g_ref, wr_ref, br_ref, c0_ref, m_ref, route_ref, cnt_ref, carry_ref):
    tm = h_ref.shape[0]
    n_exp = wr_ref.shape[1]

    @pl.when(pl.program_id(0) == 0)
    def _():
        carry_ref[...] = c0_ref[...]

    h = h_ref[...]
    m = h * _rms_scale(h) * g_ref[...]
    m_ref[...] = m
    logits = _mm(m, wr_ref[...], exact=True) + br_ref[...]

    lane = lax.broadcasted_iota(jnp.int32, (tm, n_exp), 1)
    work = logits
    vals, sels, idxs = [], [], []
    for _ in range(TOP_K):
        mx = jnp.max(work, axis=-1, keepdims=True)
        idx = jnp.min(jnp.where(work == mx, lane, n_exp), axis=-1, keepdims=True)
        sel = lane == idx
        vals.append(mx)
        idxs.append(idx)
        sels.append(sel)
        work = jnp.where(sel, -jnp.inf, work)
    es = [jnp.exp(v - vals[0]) for v in vals]
    inv = 1.0 / functools.reduce(lambda a, b: a + b, es)

    hot = functools.reduce(lambda a, b: a + b, [s.astype(F32) for s in sels])
    ri = lax.broadcasted_iota(jnp.int32, (tm, tm), 0)
    rj = lax.broadcasted_iota(jnp.int32, (tm, tm), 1)
    earlier = (rj < ri).astype(BF16)
    base = jnp.dot(earlier, hot.astype(BF16), preferred_element_type=F32) + carry_ref[...]
    ranks = [jnp.sum(jnp.where(s, base, 0.0), axis=-1, keepdims=True) for s in sels]
    carry_ref[...] = carry_ref[...] + jnp.sum(hot, axis=0, keepdims=True)
    cnt_ref[...] = carry_ref[...]

    ol = lax.broadcasted_iota(jnp.int32, (tm, ROUTE_LANES), 1)
    out = jnp.zeros((tm, ROUTE_LANES), F32)
    for k in range(TOP_K):
        out = jnp.where(ol == k, idxs[k].astype(F32), out)
        out = jnp.where(ol == TOP_K + k, ranks[k], out)
        out = jnp.where(ol == 2 * TOP_K + k, es[k] * inv, out)
    route_ref[...] = out


def _router(h, g, w_r, b_r, counts0, *, tm, name):
    t, d = h.shape
    n_exp = w_r.shape[1]
    return pl.pallas_call(
        _router_kernel,
        grid=(t // tm,),
        in_specs=[
            pl.BlockSpec((tm, d), lambda i: (i, 0)),
            pl.BlockSpec((1, d), lambda i: (0, 0)),
            pl.BlockSpec((d, n_exp), lambda i: (0, 0)),
            pl.BlockSpec((1, n_exp), lambda i: (0, 0)),
            pl.BlockSpec((1, n_exp), lambda i: (0, 0)),
        ],
        out_specs=[
            pl.BlockSpec((tm, d), lambda i: (i, 0)),
            pl.BlockSpec((tm, ROUTE_LANES), lambda i: (i, 0)),
            pl.BlockSpec((1, n_exp), lambda i: (0, 0)),
        ],
        out_shape=[
            jax.ShapeDtypeStruct((t, d), F32),
            jax.ShapeDtypeStruct((t, ROUTE_LANES), F32),
            jax.ShapeDtypeStruct((1, n_exp), F32),
        ],
        scratch_shapes=[pltpu.VMEM((1, n_exp), F32)],
        compiler_params=_params("arbitrary"),
        name=name,
    )(h, g.reshape(1, d), w_r, b_r.reshape(1, n_exp), counts0)


def _scatter_kernel(*refs, zero_fill, n_exp):
    if zero_fill:
        pstart_ref, pend_ref, dest_ref, m_ref, rows_hbm, zeros_ref, sem = refs
    else:
        pstart_ref, pend_ref, dest_ref, m_ref, _, rows_hbm, zeros_ref, sem = refs
    ts = dest_ref.shape[0] // TOP_K
    i = pl.program_id(0)

    if zero_fill:
        @pl.when(i == 0)
        def _():
            zeros_ref[...] = jnp.zeros_like(zeros_ref)

            def fill(e):
                return pltpu.make_async_copy(
                    zeros_ref, rows_hbm.at[pl.ds(pl.multiple_of(pend_ref[e] - MOE_CHUNK, MOE_CHUNK), MOE_CHUNK)], sem)

            def start(e, carry):
                @pl.when(pend_ref[e] > pstart_ref[e])
                def _():
                    fill(e).start()
                return carry

            def wait(e, carry):
                @pl.when(pend_ref[e] > pstart_ref[e])
                def _():
                    fill(e).wait()
                return carry

            lax.fori_loop(0, n_exp, start, 0)
            lax.fori_loop(0, n_exp, wait, 0)

    def row_copy(src, dst):
        return pltpu.make_async_copy(m_ref.at[pl.ds(src, 1)], rows_hbm.at[pl.ds(dst, 1)], sem)

    def start(t, carry):
        for k in range(TOP_K):
            row_copy(t, dest_ref[TOP_K * t + k]).start()
        return carry

    def wait(t, carry):
        for k in range(TOP_K):
            row_copy(0, 0).wait()
        return carry

    unroll = min(ts, 8)
    lax.fori_loop(0, ts, start, 0, unroll=unroll)
    lax.fori_loop(0, ts, wait, 0, unroll=unroll)


def _scatter_rows(m, dest, pstart, pend, rows_in, *, ts, n_rows, name):
    t, d = m.shape
    zero_fill = rows_in is None
    n_exp = pstart.shape[0]
    in_specs = [
        pl.BlockSpec((ts * TOP_K,), lambda i, *_: (i,), memory_space=pltpu.SMEM),
        pl.BlockSpec((ts, d), lambda i, *_: (i, 0)),
    ]
    args = [dest, m]
    aliases = {}
    if not zero_fill:
        in_specs.append(pl.BlockSpec(memory_space=pl.ANY))
        args.append(rows_in)
        aliases = {4: 0}
    return pl.pallas_call(
        functools.partial(_scatter_kernel, zero_fill=zero_fill, n_exp=n_exp),
        grid_spec=pltpu.PrefetchScalarGridSpec(
            num_scalar_prefetch=2,
            grid=(t // ts,),
            in_specs=in_specs,
            out_specs=pl.BlockSpec(memory_space=pl.ANY),
            scratch_shapes=[pltpu.VMEM((MOE_CHUNK, d), F32), pltpu.SemaphoreType.DMA(())],
        ),
        out_shape=jax.ShapeDtypeStruct((n_rows, d), F32),
        input_output_aliases=aliases,
        compiler_params=_params("arbitrary"),
        name=name,
    )(pstart, pend, *args)


def _expert_kernel(ir_ref, inch_ref, wge_ref, wgt_ref, wde_ref, wdt_ref,
                   rows_hbm, wg_ref, wl_ref, wd_ref, bg_ref, bl_ref, bd_ref, ys_hbm,
                   stage, xb, hdn, ybuf, sem_in, sem_out, *, n_ff, n_dt):
    i = pl.program_id(0)
    s = pl.program_id(1)
    n_items = pl.num_programs(0)
    nch = inch_ref[i]
    tile = ybuf.shape[2]

    def chunk(c):
        return slice(c * MOE_CHUNK, (c + 1) * MOE_CHUNK)

    def row_window(item, c):
        return pl.ds(pl.multiple_of(ir_ref[item], MOE_CHUNK) + c * MOE_CHUNK, MOE_CHUNK)

    def load(item, c):
        return pltpu.make_async_copy(rows_hbm.at[row_window(item, c)], stage.at[chunk(c)], sem_in.at[c])

    def store(item, c, n):
        return pltpu.make_async_copy(
            ybuf.at[n, chunk(c)], ys_hbm.at[row_window(item, c), pl.ds(n * tile, tile)], sem_out.at[c])

    def for_chunks(count, fn):
        for c in range(MOE_MAX_CHUNKS):
            pl.when(c < count)(functools.partial(fn, c))

    def for_sizes(fn):
        for nc in range(1, MOE_MAX_CHUNKS + 1):
            pl.when(nch == nc)(functools.partial(fn, nc * MOE_CHUNK))

    @pl.when(s == 0)
    def _():
        @pl.when(i == 0)
        def _():
            for_chunks(nch, lambda c: load(0, c).start())

        def land(c):
            load(i, c).wait()
            xb[chunk(c), :] = stage[chunk(c), :].astype(BF16)

        for_chunks(nch, land)

    @pl.when((s < n_ff) & (nch > 0))
    def _():
        wg = wg_ref[...].astype(BF16)
        wl = wl_ref[...].astype(BF16)

        def phase1(rows):
            x = xb[:rows, :]
            gate = jnp.dot(x, wg, preferred_element_type=F32) + bg_ref[...]
            lin = jnp.dot(x, wl, preferred_element_type=F32) + bl_ref[...]
            gate = jnp.minimum(gate, SWIGLU_LIMIT)
            lin = jnp.clip(lin, -SWIGLU_LIMIT, SWIGLU_LIMIT)
            act = gate * jax.nn.sigmoid(SWIGLU_ALPHA * gate) * (lin + 1.0)
            hdn[s, :rows, :] = act.astype(BF16)

        for_sizes(phase1)

    @pl.when(s == n_ff)
    def _():
        @pl.when(i > 0)
        def _():
            def drain(c):
                for n in range(n_dt):
                    store(i - 1, c, n).wait()

            for_chunks(inch_ref[jnp.maximum(i - 1, 0)], drain)

        @pl.when(i + 1 < n_items)
        def _():
            nxt = jnp.minimum(i + 1, n_items - 1)
            for_chunks(inch_ref[nxt], lambda c: load(nxt, c).start())

    @pl.when((s >= n_ff) & (nch > 0))
    def _():
        wd = wd_ref[...].astype(BF16)
        n = s - n_ff

        def phase2(rows):
            y = bd_ref[...]
            for f in range(n_ff):
                y = y + jnp.dot(hdn[f, :rows, :], wd[f * tile:(f + 1) * tile, :],
                                preferred_element_type=F32)
            ybuf[n, :rows, :] = y

        for_sizes(phase2)

    @pl.when(s == n_ff + n_dt - 1)
    def _():
        def flush(c):
            for n in range(n_dt):
                store(i, c, n).start()

        for_chunks(nch, flush)

        @pl.when(i == n_items - 1)
        def _():
            def drain(c):
                for n in range(n_dt):
                    store(i, c, n).wait()

            for_chunks(nch, drain)


def _expert_ffn(rows, w_gu, b_gu, w_down, b_down, item_e, item_row0, item_nch):
    n_rows, d = rows.shape
    n_exp, _, two_ff = w_gu.shape
    d_ff = two_ff // 2
    tile = MOE_FF_TILE
    n_ff = d_ff // tile
    n_dt = d // tile
    n_steps = n_ff + n_dt
    n_items = item_e.shape[0]
    r_max = MOE_MAX_CHUNKS * MOE_CHUNK

    valid = (item_nch > 0)[:, None]
    last = jnp.sum(item_nch > 0) - 1
    e_last = item_e[last]
    e_next = item_e[jnp.minimum(jnp.arange(n_items) + 1, last)][:, None]
    step = jnp.arange(n_steps, dtype=jnp.int32)[None, :]
    first = step < n_ff
    e_col = item_e[:, None]
    wg_e = jnp.where(valid, jnp.where(first, e_col, e_next), e_last).reshape(-1)
    wg_t = jnp.where(valid & first, step, 0).reshape(-1)
    wd_e = jnp.broadcast_to(jnp.where(valid, e_col, e_last), (n_items, n_steps)).reshape(-1)
    wd_t = jnp.where(valid, jnp.where(first, 0, step - n_ff), n_dt - 1).reshape(-1)

    def at(i, s):
        return i * n_steps + s

    def gate_map(i, s, ir, inch, wge, wgt, wde, wdt):
        return wge[at(i, s)], 0, wgt[at(i, s)]

    def lin_map(i, s, ir, inch, wge, wgt, wde, wdt):
        return wge[at(i, s)], 0, n_ff + wgt[at(i, s)]

    def down_map(i, s, ir, inch, wge, wgt, wde, wdt):
        return wde[at(i, s)], 0, wdt[at(i, s)]

    return pl.pallas_call(
        functools.partial(_expert_kernel, n_ff=n_ff, n_dt=n_dt),
        grid_spec=pltpu.PrefetchScalarGridSpec(
            num_scalar_prefetch=6,
            grid=(n_items, n_steps),
            in_specs=[
                pl.BlockSpec(memory_space=pl.ANY),
                pl.BlockSpec((None, d, tile), gate_map),
                pl.BlockSpec((None, d, tile), lin_map),
                pl.BlockSpec((None, d_ff, tile), down_map),
                pl.BlockSpec((None, 1, tile), gate_map),
                pl.BlockSpec((None, 1, tile), lin_map),
                pl.BlockSpec((None, 1, tile), down_map),
            ],
            out_specs=pl.BlockSpec(memory_space=pl.ANY),
            scratch_shapes=[
                pltpu.VMEM((r_max, d), F32),
                pltpu.VMEM((r_max, d), BF16),
                pltpu.VMEM((n_ff, r_max, tile), BF16),
                pltpu.VMEM((n_dt, r_max, tile), F32),
                pltpu.SemaphoreType.DMA((MOE_MAX_CHUNKS,)),
                pltpu.SemaphoreType.DMA((MOE_MAX_CHUNKS,)),
            ],
        ),
        out_shape=jax.ShapeDtypeStruct((n_rows, d), F32),
        compiler_params=_params("arbitrary", "arbitrary"),
        name="expert_ffn",
    )(item_row0, item_nch, wg_e.astype(jnp.int32), wg_t.astype(jnp.int32),
      wd_e.astype(jnp.int32), wd_t.astype(jnp.int32), rows, w_gu, w_gu, w_down,
      b_gu.reshape(n_exp, 1, two_ff), b_gu.reshape(n_exp, 1, two_ff),
      b_down.reshape(n_exp, 1, d))


def _combine_kernel(dest_ref, route_ref, h_ref, ys_hbm, o_ref, buf, sem):
    tg = h_ref.shape[0]

    def row_copy(src, k, t):
        return pltpu.make_async_copy(ys_hbm.at[pl.ds(src, 1)], buf.at[k, pl.ds(t, 1)], sem)

    def start(t, carry):
        for k in range(TOP_K):
            row_copy(dest_ref[TOP_K * t + k], k, t).start()
        return carry

    def wait(t, carry):
        for k in range(TOP_K):
            row_copy(0, k, 0).wait()
        return carry

    unroll = min(tg, 8)
    lax.fori_loop(0, tg, start, 0, unroll=unroll)
    lax.fori_loop(0, tg, wait, 0, unroll=unroll)
    out = h_ref[...]
    for k in range(TOP_K):
        out = out + route_ref[:, 2 * TOP_K + k:2 * TOP_K + k + 1] * buf[k]
    o_ref[...] = out


def _combine(dest, route, h, ys, *, tg, name):
    t, d = h.shape
    return pl.pallas_call(
        _combine_kernel,
        grid=(t // tg,),
        in_specs=[
            pl.BlockSpec((tg * TOP_K,), lambda i: (i,), memory_space=pltpu.SMEM),
            pl.BlockSpec((tg, ROUTE_LANES), lambda i: (i, 0)),
            pl.BlockSpec((tg, d), lambda i: (i, 0)),
            pl.BlockSpec(memory_space=pl.ANY),
        ],
        out_specs=pl.BlockSpec((tg, d), lambda i: (i, 0)),
        out_shape=jax.ShapeDtypeStruct((t, d), F32),
        scratch_shapes=[pltpu.VMEM((TOP_K, tg, d), F32), pltpu.SemaphoreType.DMA(())],
        compiler_params=_params("arbitrary"),
        name=name,
    )(dest, route, h, ys)


def _ple_kernel(h_ref, hc_ref, p_ref, g_ref, wg_ref, wp_ref, o_ref, hn_ref):
    @pl.when(pl.program_id(1) == 0)
    def _():
        rows = min(h_ref.shape[0], 128)

        def body(c, carry):
            sl = pl.ds(pl.multiple_of(c * rows, rows), rows)
            h = h_ref[sl, :]
            hn_ref[sl, :] = (h * _rms_scale(h) * g_ref[...]).astype(BF16)
            return carry

        lax.fori_loop(0, h_ref.shape[0] // rows, body, 0)

    gate = jax.nn.sigmoid(jnp.dot(hn_ref[...], wg_ref[...].astype(BF16), preferred_element_type=F32))
    proj = jnp.dot(p_ref[...].astype(BF16), wp_ref[...].astype(BF16), preferred_element_type=F32)
    o_ref[...] = hc_ref[...] + gate * proj


def _ple(h, p, g, w_gate, w_proj, *, tm, tn, name):
    t, d = h.shape
    pd = p.shape[1]
    return pl.pallas_call(
        _ple_kernel,
        grid=(t // tm, d // tn),
        in_specs=[
            pl.BlockSpec((tm, d), lambda i, j: (i, 0)),
            pl.BlockSpec((tm, tn), lambda i, j: (i, j)),
            pl.BlockSpec((tm, pd), lambda i, j: (i, 0)),
            pl.BlockSpec((1, d), lambda i, j: (0, 0)),
            pl.BlockSpec((d, tn), lambda i, j: (0, j)),
            pl.BlockSpec((pd, tn), lambda i, j: (0, j)),
        ],
        out_specs=pl.BlockSpec((tm, tn), lambda i, j: (i, j)),
        out_shape=jax.ShapeDtypeStruct((t, d), F32),
        scratch_shapes=[pltpu.VMEM((tm, d), BF16)],
        compiler_params=_params("parallel", "arbitrary"),
        name=name,
    )(h, h, p, g.reshape(1, d), w_gate, w_proj)


def _decode_kernel(*refs, groups, hpg):
    ng = len(groups)
    z_ref = refs[0]
    caches = refs[1:1 + ng]
    (cos_ref, sa_ref, sb_ref, gq_ref, gk_ref, lng_ref, lnb_ref, ws0_ref, bs0_ref,
     attn_ref, kn_ref, gm_ref, vn_ref) = refs[1 + ng:]
    heads = ng * hpg
    aw = heads * HEAD_DIM
    scale = HEAD_DIM ** -0.5
    tabs = (cos_ref[...], sa_ref[...], sb_ref[...])

    def head(base, hh):
        return z_ref[:, base + hh * HEAD_DIM:base + (hh + 1) * HEAD_DIM]

    outs, lses = [], []
    for g in range(ng):
        c_ref = caches[g]
        for j in range(hpg):
            hh = g * hpg + j
            qn = _norm_rope(head(0, hh), gq_ref[...], *tabs) * scale
            kn = _norm_rope(head(aw, hh), gk_ref[...], *tabs)
            vv = head(2 * aw, hh)
            kn_ref[:, hh * HEAD_DIM:(hh + 1) * HEAD_DIM] = kn
            kc = c_ref[:, j * HEAD_DIM:(j + 1) * HEAD_DIM]
            vc = c_ref[:, (hpg + j) * HEAD_DIM:(hpg + j + 1) * HEAD_DIM]
            s_c = jnp.sum(kc * qn, axis=-1, keepdims=True)
            s_n = jnp.sum(kn * qn, axis=-1, keepdims=True)
            m = jnp.maximum(jnp.max(s_c, axis=0, keepdims=True), s_n)
            p_c = jnp.exp(s_c - m)
            p_n = jnp.exp(s_n - m)
            den = jnp.sum(p_c, axis=0, keepdims=True) + p_n
            o = (jnp.sum(p_c * vc, axis=0, keepdims=True) + p_n * vv) / den
            outs.append(o)
            lses.append(m + jnp.log(den))

    for j in range(hpg):
        ls = [lses[g * hpg + j] for g in range(ng)]
        mx = functools.reduce(jnp.maximum, ls)
        es = [jnp.exp(l - mx) for l in ls]
        inv = 1.0 / functools.reduce(lambda a, b: a + b, es)
        for g in range(ng):
            hh = g * hpg + j
            attn_ref[:, hh * HEAD_DIM:(hh + 1) * HEAD_DIM] = outs[hh] * (es[g] * inv)

    gw = gm_ref.shape[1]
    for g in range(gw // GMLP_CH):
        cols = slice(g * GMLP_CH, (g + 1) * GMLP_CH)
        v = _gelu(z_ref[:, 3 * aw + gw + g * GMLP_CH:3 * aw + gw + (g + 1) * GMLP_CH])
        mu = jnp.mean(v, axis=-1, keepdims=True)
        vc = v - mu
        var = jnp.mean(vc * vc, axis=-1, keepdims=True)
        vn = vc * lax.rsqrt(var + EPS) * lng_ref[:, cols] + lnb_ref[:, cols]
        vn_ref[:, cols] = vn
        u = _gelu(z_ref[:, 3 * aw + g * GMLP_CH:3 * aw + (g + 1) * GMLP_CH])
        gm_ref[:, cols] = u * (ws0_ref[:, cols] * vn + bs0_ref[:, cols])


def _sample_mixers(zs, caches, tables, g_q, g_k, ln_g, ln_b, w_s, b_s, *, hpg):
    bsz, in_width = zs.shape
    ng = len(DILATION_GROUPS)
    aw = ng * hpg * HEAD_DIM
    gw = (in_width - 3 * aw) // 2
    row = 2 * hpg * HEAD_DIM
    in_specs = [pl.BlockSpec((None, 1, in_width), lambda b: (b, 0, 0))]
    cache_args = []
    for (window, dil), c in zip(DILATION_GROUPS, caches):
        wc = c.shape[1]
        assert wc == window and c.shape[0] == bsz, "window caches must hold a full window"
        steps = wc // dil
        cache_args.append(c.reshape(bsz, steps, dil * row))
        in_specs.append(pl.BlockSpec((None, steps, row), lambda b: (b, 0, 0)))
    vec = pl.BlockSpec((1, HEAD_DIM), lambda b: (0, 0))
    wide = pl.BlockSpec((1, gw), lambda b: (0, 0))
    in_specs += [vec] * 5 + [wide] * 4

    def out(w):
        return pl.BlockSpec((None, 1, w), lambda b: (b, 0, 0)), jax.ShapeDtypeStruct((bsz, 1, w), F32)

    specs, shapes = zip(out(aw), out(aw), out(gw), out(gw))
    groups = w_s.shape[0]
    ws0 = jnp.repeat(w_s[:, 0, 0], GMLP_CH).reshape(1, gw)
    bs0 = jnp.repeat(b_s[:, 0], GMLP_CH).reshape(1, gw)
    assert groups * GMLP_CH == gw
    res = pl.pallas_call(
        functools.partial(_decode_kernel, groups=DILATION_GROUPS, hpg=hpg),
        grid=(bsz,),
        in_specs=in_specs,
        out_specs=list(specs),
        out_shape=list(shapes),
        compiler_params=_params("parallel"),
        name="sample_mixers",
    )(zs.reshape(bsz, 1, in_width), *cache_args, *tables,
      g_q.reshape(1, HEAD_DIM), g_k.reshape(1, HEAD_DIM),
      ln_g.reshape(1, gw), ln_b.reshape(1, gw), ws0, bs0)
    return [r.reshape(bsz, -1) for r in res]


def _expert_layout(counts, n_items):
    counts = counts.astype(jnp.int32)
    padded = (counts + MOE_CHUNK - 1) // MOE_CHUNK * MOE_CHUNK
    pend = jnp.cumsum(padded)
    pstart = pend - padded
    r_max = MOE_MAX_CHUNKS * MOE_CHUNK
    per_e = (padded + r_max - 1) // r_max
    item_end = jnp.cumsum(per_e)
    total = item_end[-1]
    ids = jnp.arange(n_items, dtype=jnp.int32)
    clamped = jnp.minimum(ids, total - 1)
    e = jnp.sum((item_end[None, :] <= clamped[:, None]).astype(jnp.int32), axis=1)
    local = clamped - (item_end[e] - per_e[e])
    row0 = pstart[e] + local * r_max
    nch = jnp.minimum(padded[e] - local * r_max, r_max) // MOE_CHUNK
    nch = jnp.where(ids < total, nch, 0)
    return pstart, pend, e, row0.astype(jnp.int32), nch.astype(jnp.int32)


def _destinations(route, pstart):
    e = route[:, :TOP_K].astype(jnp.int32)
    rank = route[:, TOP_K:2 * TOP_K].astype(jnp.int32)
    return (pstart[e] + rank).reshape(-1)


def kernel(x_prompt, x_sample, cache_kv_w128, cache_kv_w512, cache_kv_w2048, p_prompt, p_sample,
           g_mix, w_in, g_q, g_k, gmlp_ln_g, gmlp_ln_b, gmlp_w_s, gmlp_b_s, g_out_attn, g_out_gmlp,
           w_out, g_ffn, w_router, b_router, w_gu, b_gu, w_down, b_down, g_ple, w_ple_gate,
           w_ple_proj):
    depth = g_mix.shape[0]
    assert depth == 1, "single-layer step"
    batch, seq, d = x_prompt.shape
    dec_batch, dec_seq, _ = x_sample.shape
    assert dec_seq == 1
    in_width = w_in.shape[2]
    gw = gmlp_ln_g.shape[1] * gmlp_ln_g.shape[2]
    aw = (in_width - 2 * gw) // 3
    ng = len(DILATION_GROUPS)
    hpg = aw // HEAD_DIM // ng
    n_exp = w_router.shape[2]
    tp = batch * seq
    ts = dec_batch * dec_seq

    def layer(a):
        return a.reshape(a.shape[1:])

    caches = [layer(c) for c in (cache_kv_w128, cache_kv_w512, cache_kv_w2048)]
    caches = [c.reshape(c.shape[0], c.shape[1], -1) for c in caches]
    (p_prompt, p_sample, g_mix, w_in, g_q, g_k, gmlp_ln_g, gmlp_ln_b, gmlp_w_s, gmlp_b_s,
     g_out_attn, g_out_gmlp, w_out, g_ffn, w_router, b_router, w_gu, b_gu, w_down, b_down, g_ple,
     w_ple_gate, w_ple_proj) = [layer(a) for a in (
         p_prompt, p_sample, g_mix, w_in, g_q, g_k, gmlp_ln_g, gmlp_ln_b, gmlp_w_s, gmlp_b_s,
         g_out_attn, g_out_gmlp, w_out, g_ffn, w_router, b_router, w_gu, b_gu, w_down, b_down,
         g_ple, w_ple_gate, w_ple_proj)]

    xp = x_prompt.reshape(tp, d)
    z = _norm_matmul(xp, g_mix, w_in, tm=1024, tn=512, name="prompt_in_proj")
    z3 = z.reshape(batch, seq, in_width)
    tables_p = _rope_tables(jnp.arange(seq, dtype=jnp.int32))
    attn_p, kn_p = _prompt_attention(z3, g_q, g_k, tables_p, batch=batch, seq=seq, hpg=hpg)
    gm_p = _prompt_gmlp(z3, gmlp_ln_g, gmlp_ln_b, gmlp_w_s, gmlp_b_s,
                        batch=batch, seq=seq, width=gw, col_block=3 * aw // gw)
    hp = _out_proj([a.reshape(tp, -1) for a in attn_p], gm_p.reshape(tp, gw), xp,
                   g_out_attn, g_out_gmlp, w_out, tm=1024, tn=512, name="prompt_out_proj")

    xs = x_sample.reshape(ts, d)
    zs = _norm_matmul(xs, g_mix, w_in, tm=ts, tn=512, name="sample_in_proj", exact=True)
    tables_s = _rope_tables(PAST_LEN + jnp.arange(dec_seq, dtype=jnp.int32))
    attn_s, kn_s, gm_s, vn_s = _sample_mixers(
        zs, caches, tables_s, g_q, g_k, gmlp_ln_g, gmlp_ln_b, gmlp_w_s, gmlp_b_s, hpg=hpg)
    gwid = hpg * HEAD_DIM
    hs = _out_proj([attn_s[:, g * gwid:(g + 1) * gwid] for g in range(ng)], gm_s, xs,
                   g_out_attn, g_out_gmlp, w_out, tm=ts, tn=512, name="sample_out_proj",
                   exact=True)

    zero_counts = jnp.zeros((1, n_exp), F32)
    m_p, route_p, counts_p = _router(hp, g_ffn, w_router, b_router, zero_counts,
                                     tm=512, name="prompt_router")
    m_s, route_s, counts = _router(hs, g_ffn, w_router, b_router, counts_p,
                                   tm=ts, name="sample_router")
    n_assign = (tp + ts) * TOP_K
    n_chunks = -(-n_assign // MOE_CHUNK) + n_exp
    n_rows = n_chunks * MOE_CHUNK
    n_items = n_exp + -(-n_chunks // MOE_MAX_CHUNKS)
    pstart, pend, item_e, item_row0, item_nch = _expert_layout(counts.reshape(n_exp), n_items)
    dest_p = _destinations(route_p, pstart)
    dest_s = _destinations(route_s, pstart)
    rows = _scatter_rows(m_p, dest_p, pstart, pend, None, ts=512, n_rows=n_rows,
                         name="prompt_scatter")
    rows = _scatter_rows(m_s, dest_s, pstart, pend, rows, ts=ts, n_rows=n_rows,
                         name="sample_scatter")
    ys = _expert_ffn(rows, w_gu, b_gu, w_down, b_down, item_e, item_row0, item_nch)
    hp = _combine(dest_p, route_p, hp, ys, tg=256, name="prompt_combine")
    hs = _combine(dest_s, route_s, hs, ys, tg=ts, name="sample_combine")

    yp = _ple(hp, p_prompt.reshape(tp, -1), g_ple, w_ple_gate, w_ple_proj,
              tm=1024, tn=512, name="prompt_ple")
    ysm = _ple(hs, p_sample.reshape(ts, -1), g_ple, w_ple_gate, w_ple_proj,
               tm=ts, tn=512, name="sample_ple")

    v0 = 2 * aw
    kv_prompt, kv_sample = [], []
    for g, (window, _) in enumerate(DILATION_GROUPS):
        wc = min(window, seq)
        kg = kn_p[g][:, seq - wc:].reshape(batch, wc, hpg, HEAD_DIM)
        vg = z3[:, seq - wc:, v0 + g * gwid:v0 + (g + 1) * gwid].reshape(batch, wc, hpg, HEAD_DIM)
        kv_prompt.append(jnp.stack([kg, vg], axis=2)[None])
        ksg = kn_s[:, g * gwid:(g + 1) * gwid].reshape(dec_batch, dec_seq, hpg, HEAD_DIM)
        vsg = zs[:, v0 + g * gwid:v0 + (g + 1) * gwid].reshape(dec_batch, dec_seq, hpg, HEAD_DIM)
        kv_sample.append(jnp.stack([ksg, vsg], axis=2)[None])
    gmlp_v = vn_s.reshape(1, dec_batch, dec_seq, gw // GMLP_CH, GMLP_CH)
    return (yp.reshape(batch, seq, d), ysm.reshape(dec_batch, dec_seq, d),
            *kv_prompt, *kv_sample, gmlp_v)
```

```python
import functools

import jax
import jax.numpy as jnp
from jax import lax
from jax.experimental import pallas as pl
from jax.experimental.pallas import tpu as pltpu

F32 = jnp.float32
BF16 = jnp.bfloat16

EPS = 1e-6
HEAD_DIM = 128
ROT_DIM = HEAD_DIM // 4
ROPE_THETA = 500000.0
DILATION_GROUPS = ((128, 1), (512, 4), (2048, 16))
PAST_LEN = 16384
QBLK = 128
CHUNK = 128
GMLP_CH = 128
TOP_K = 4
SWIGLU_LIMIT = 7.0
SWIGLU_ALPHA = 1.702

V7X_VMEM_BYTES = 64 * 1024 * 1024
VMEM_LIMIT = V7X_VMEM_BYTES - 8 * 1024 * 1024

MOE_CHUNK = 256
MOE_MAX_CHUNKS = 5
MOE_FF_TILE = 256


def _params(*sem):
    return pltpu.CompilerParams(dimension_semantics=sem, vmem_limit_bytes=VMEM_LIMIT)


def _split_bf16(a):
    hi = a.astype(BF16)
    lo = (a - hi.astype(F32)).astype(BF16)
    return hi, lo


def _mm(a, b, exact=False):
    d = functools.partial(jnp.dot, preferred_element_type=F32)
    if not exact:
        return d(a.astype(BF16), b.astype(BF16))
    ah, al = _split_bf16(a)
    bh, bl = _split_bf16(b)
    return d(ah, bh) + (d(ah, bl) + d(al, bh))


def _gelu(x):
    return 0.5 * x * (1.0 + jnp.tanh(0.7978845608028654 * (x + 0.044715 * (x * x * x))))


def _rms_scale(x):
    return lax.rsqrt(jnp.mean(x * x, axis=-1, keepdims=True) + EPS)


def _norm_matmul_kernel(x_ref, g_ref, w_ref, o_ref, xn_ref, *, exact, rows_inner):
    def normalise():
        rows = min(x_ref.shape[0], 128)

        def body(c, carry):
            sl = pl.ds(pl.multiple_of(c * rows, rows), rows)
            x = x_ref[sl, :]
            xn_ref[sl, :] = (x * _rms_scale(x) * g_ref[...]).astype(xn_ref.dtype)
            return carry

        lax.fori_loop(0, x_ref.shape[0] // rows, body, 0)

    if rows_inner:
        normalise()
    else:
        pl.when(pl.program_id(1) == 0)(normalise)

    o_ref[...] = _mm(xn_ref[...], w_ref[...], exact)


def _norm_matmul(x, g, w, *, tm, tn, name, exact=False, rows_inner=False):
    m, k = x.shape
    n = w.shape[1]
    if rows_inner:
        grid = (n // tn, m // tm)
        row, col = (lambda j, i: i), (lambda j, i: j)
    else:
        grid = (m // tm, n // tn)
        row, col = (lambda i, j: i), (lambda i, j: j)
    return pl.pallas_call(
        functools.partial(_norm_matmul_kernel, exact=exact, rows_inner=rows_inner),
        grid=grid,
        in_specs=[
            pl.BlockSpec((tm, k), lambda a, b: (row(a, b), 0)),
            pl.BlockSpec((1, k), lambda a, b: (0, 0)),
            pl.BlockSpec((k, tn), lambda a, b: (0, col(a, b))),
        ],
        out_specs=pl.BlockSpec((tm, tn), lambda a, b: (row(a, b), col(a, b))),
        out_shape=jax.ShapeDtypeStruct((m, n), F32),
        scratch_shapes=[pltpu.VMEM((tm, k), F32 if exact else BF16)],
        compiler_params=_params("parallel", "arbitrary"),
        name=name,
    )(x, g.reshape(1, k), w)


def _rope_tables(pos):
    half = ROT_DIM // 2
    inv = 1.0 / (ROPE_THETA ** (jnp.arange(half, dtype=F32) * (2.0 / ROT_DIM)))
    ang = pos.astype(F32)[:, None] * inv[None, :]
    cos, sin = jnp.cos(ang), jnp.sin(ang)
    t = pos.shape[0]
    pad = HEAD_DIM - ROT_DIM
    cos_t = jnp.concatenate([cos, cos, jnp.ones((t, pad), F32)], axis=1)
    sa = jnp.concatenate([-sin, jnp.zeros((t, HEAD_DIM - half), F32)], axis=1)
    sb = jnp.concatenate([jnp.zeros((t, half), F32), sin, jnp.zeros((t, pad), F32)], axis=1)
    return cos_t, sa, sb


def _norm_rope(x, gain, cos_t, sa, sb):
    half = ROT_DIM // 2
    xn = x * _rms_scale(x) * gain
    up = pltpu.roll(xn, HEAD_DIM - half, 1)
    dn = pltpu.roll(xn, half, 1)
    return xn * cos_t + up * sa + dn * sb


def _band_block(q, k, v, mask):
    s = lax.dot_general(q.astype(BF16), k.astype(BF16), (((1,), (1,)), ((), ())),
                        preferred_element_type=F32)
    s = jnp.where(mask, s, -jnp.inf)
    m = jnp.max(s, axis=-1, keepdims=True)
    p = jnp.exp(s - m)
    den = jnp.sum(p, axis=-1, keepdims=True)
    o = jnp.dot((p / den).astype(BF16), v.astype(BF16), preferred_element_type=F32)
    return o, m + jnp.log(den)


def _attn_kernel(*refs, seq, groups):
    ng = len(groups)
    qkv = refs[:3 * ng]
    cos_ref, sa_ref, sb_ref, gq_ref, gk_ref = refs[3 * ng:3 * ng + 5]
    outs = refs[3 * ng + 5:3 * ng + 5 + 2 * ng]
    o_refs, kn_refs = outs[:ng], outs[ng:]
    qs, ks = refs[3 * ng + 5 + 2 * ng:3 * ng + 7 + 2 * ng]
    ls_refs = refs[3 * ng + 7 + 2 * ng:]
    scale = HEAD_DIM ** -0.5
    nrm_rows = 256

    qi = lax.broadcasted_iota(jnp.int32, (QBLK, 2 * QBLK), 0)
    kj = lax.broadcasted_iota(jnp.int32, (QBLK, 2 * QBLK), 1)
    qi1 = lax.broadcasted_iota(jnp.int32, (QBLK, QBLK), 0)
    kj1 = lax.broadcasted_iota(jnp.int32, (QBLK, QBLK), 1)

    for g, (window, dil) in enumerate(groups):
        q_ref, k_ref, v_ref = qkv[3 * g:3 * g + 3]
        o_ref, kn_ref, ls_ref = o_refs[g], kn_refs[g], ls_refs[g]
        n_steps = window // dil
        nb = seq // dil // QBLK

        def prep(c, carry, q_ref=q_ref, k_ref=k_ref, kn_ref=kn_ref):
            sl = pl.ds(pl.multiple_of(c * nrm_rows, nrm_rows), nrm_rows)
            tabs = (cos_ref[sl, :], sa_ref[sl, :], sb_ref[sl, :])
            qs[sl, :] = _norm_rope(q_ref[sl, :], gq_ref[...], *tabs) * scale
            kn = _norm_rope(k_ref[sl, :], gk_ref[...], *tabs)
            kn_ref[sl, :] = kn
            ks[sl, :] = kn
            return carry

        lax.fori_loop(0, seq // nrm_rows, prep, 0)

        def rows(start, size, dil=dil):
            return pl.ds(start, size) if dil == 1 else pl.ds(start, size, stride=dil)

        band_mask = (kj >= qi + (QBLK - n_steps)) & (kj <= qi + QBLK)
        first_mask = (kj1 <= qi1) & (qi1 - kj1 <= n_steps)

        def one_class(r, carry, v_ref=v_ref, o_ref=o_ref, ls_ref=ls_ref, rows=rows,
                      band_mask=band_mask, first_mask=first_mask, dil=dil, nb=nb):
            sl0 = rows(r, QBLK)
            o, lse = _band_block(qs[sl0, :], ks[sl0, :], v_ref[sl0, :], first_mask)
            o_ref[sl0, :] = o
            ls_ref[sl0, :] = jnp.broadcast_to(lse, (QBLK, HEAD_DIM))

            def one_block(lb, carry2):
                q0 = r + dil * QBLK * lb
                k0 = q0 - dil * QBLK
                slq, slk = rows(q0, QBLK), rows(k0, 2 * QBLK)
                o, lse = _band_block(qs[slq, :], ks[slk, :], v_ref[slk, :], band_mask)
                o_ref[slq, :] = o
                ls_ref[slq, :] = jnp.broadcast_to(lse, (QBLK, HEAD_DIM))
                return carry2

            lax.fori_loop(1, nb, one_block, 0, unroll=min(5, max(nb - 1, 1)))
            return carry

        lax.fori_loop(0, dil, one_class, 0, unroll=min(8, dil) if nb == 1 else 1)

    def merge(c, carry):
        sl = pl.ds(pl.multiple_of(c * nrm_rows, nrm_rows), nrm_rows)
        ls = [ls_refs[g][sl, :] for g in range(ng)]
        mx = functools.reduce(jnp.maximum, ls)
        es = [jnp.exp(l - mx) for l in ls]
        inv = 1.0 / functools.reduce(lambda a, b: a + b, es)
        for g in range(ng):
            o_refs[g][sl, :] = o_refs[g][sl, :] * (es[g] * inv)
        return carry

    lax.fori_loop(0, seq // nrm_rows, merge, 0)


def _prompt_attention(z, g_q, g_k, tables, *, batch, seq, hpg):
    ng = len(DILATION_GROUPS)
    heads = ng * hpg
    blk = (None, seq, HEAD_DIM)

    def col(c):
        return pl.BlockSpec(blk, lambda b, h, c=c: (b, 0, c + h))

    in_specs = []
    for g in range(ng):
        in_specs += [col(g * hpg), col(heads + g * hpg), col(2 * heads + g * hpg)]
    tab = pl.BlockSpec((seq, HEAD_DIM), lambda b, h: (0, 0))
    vec = pl.BlockSpec((1, HEAD_DIM), lambda b, h: (0, 0))
    in_specs += [tab, tab, tab, vec, vec]
    out_spec = pl.BlockSpec(blk, lambda b, h: (b, 0, h))
    out_sds = jax.ShapeDtypeStruct((batch, seq, hpg * HEAD_DIM), F32)
    res = pl.pallas_call(
        functools.partial(_attn_kernel, seq=seq, groups=DILATION_GROUPS),
        grid=(batch, hpg),
        in_specs=in_specs,
        out_specs=[out_spec] * (2 * ng),
        out_shape=[out_sds] * (2 * ng),
        scratch_shapes=[pltpu.VMEM((seq, HEAD_DIM), F32)] * (2 + ng),
        compiler_params=_params("parallel", "arbitrary"),
        name="prompt_attention",
    )(*([z] * (3 * ng)), *tables, g_q.reshape(1, HEAD_DIM), g_k.reshape(1, HEAD_DIM))
    return res[:ng], res[ng:]


def _gmlp_kernel(u_ref, vg_ref, lng_ref, lnb_ref, ws_ref, bs_ref, o_ref):
    groups = ws_ref.shape[0]
    ci = lax.broadcasted_iota(jnp.int32, (CHUNK, CHUNK), 0)
    cj = lax.broadcasted_iota(jnp.int32, (CHUNK, CHUNK), 1)
    for g in range(groups):
        cols = slice(g * GMLP_CH, (g + 1) * GMLP_CH)
        ws = jnp.where(cj <= ci, ws_ref[g], 0.0).astype(BF16)
        bias = bs_ref[:, g:g + 1]
        for n in range(u_ref.shape[0] // CHUNK):
            rows = slice(n * CHUNK, (n + 1) * CHUNK)
            v = _gelu(vg_ref[rows, cols])
            mu = jnp.mean(v, axis=-1, keepdims=True)
            vc = v - mu
            var = jnp.mean(vc * vc, axis=-1, keepdims=True)
            vn = vc * lax.rsqrt(var + EPS) * lng_ref[:, cols] + lnb_ref[:, cols]
            mixed = jnp.dot(ws, vn.astype(BF16), preferred_element_type=F32) + bias
            o_ref[rows, cols] = _gelu(u_ref[rows, cols]) * mixed


def _prompt_gmlp(z, ln_g, ln_b, w_s, b_s, *, batch, seq, width, col_block):
    tc = 4 * CHUNK
    groups = w_s.shape[0]
    return pl.pallas_call(
        _gmlp_kernel,
        grid=(batch, seq // tc),
        in_specs=[
            pl.BlockSpec((None, tc, width), lambda b, n: (b, n, col_block)),
            pl.BlockSpec((None, tc, width), lambda b, n: (b, n, col_block + 1)),
            pl.BlockSpec((1, width), lambda b, n: (0, 0)),
            pl.BlockSpec((1, width), lambda b, n: (0, 0)),
            pl.BlockSpec((groups, CHUNK, CHUNK), lambda b, n: (0, 0, 0)),
            pl.BlockSpec((CHUNK, groups), lambda b, n: (0, 0)),
        ],
        out_specs=pl.BlockSpec((None, tc, width), lambda b, n: (b, n, 0)),
        out_shape=jax.ShapeDtypeStruct((batch, seq, width), F32),
        compiler_params=_params("parallel", "parallel"),
        name="prompt_gmlp",
    )(z, z, ln_g.reshape(1, width), ln_b.reshape(1, width), w_s, b_s.T)


def _out_proj_kernel(*refs, n_attn, exact):
    a_refs = refs[:n_attn]
    gm_ref, x_ref, ga_ref, gg_ref, w_ref, o_ref, yn_ref = refs[n_attn:]
    wa = a_refs[0].shape[1]

    @pl.when(pl.program_id(1) == 0)
    def _():
        rows = min(x_ref.shape[0], 128)

        def body(c, carry):
            sl = pl.ds(pl.multiple_of(c * rows, rows), rows)
            parts = [a[sl, :] for a in a_refs]
            ssq = functools.reduce(lambda p, q: p + q,
                                   [jnp.sum(p * p, axis=-1, keepdims=True) for p in parts])
            rs = lax.rsqrt(ssq / (n_attn * wa) + EPS)
            for idx, p in enumerate(parts):
                cs = slice(idx * wa, (idx + 1) * wa)
                yn_ref[sl, cs] = (p * rs * ga_ref[:, cs]).astype(yn_ref.dtype)
            gm = gm_ref[sl, :]
            yn_ref[sl, n_attn * wa:] = (gm * _rms_scale(gm) * gg_ref[...]).astype(yn_ref.dtype)
            return carry

        lax.fori_loop(0, x_ref.shape[0] // rows, body, 0)

    o_ref[...] = x_ref[...] + _mm(yn_ref[...], w_ref[...], exact)


def _out_proj(attn_parts, gm, x, g_attn, g_gmlp, w, *, tm, tn, name, exact=False):
    m, d = x.shape
    n_attn = len(attn_parts)
    wa = attn_parts[0].shape[1]
    wg = gm.shape[1]
    part = pl.BlockSpec((tm, wa), lambda i, j: (i, 0))
    return pl.pallas_call(
        functools.partial(_out_proj_kernel, n_attn=n_attn, exact=exact),
        grid=(m // tm, d // tn),
        in_specs=[part] * n_attn + [
            pl.BlockSpec((tm, wg), lambda i, j: (i, 0)),
            pl.BlockSpec((tm, tn), lambda i, j: (i, j)),
            pl.BlockSpec((1, n_attn * wa), lambda i, j: (0, 0)),
            pl.BlockSpec((1, wg), lambda i, j: (0, 0)),
            pl.BlockSpec((n_attn * wa + wg, tn), lambda i, j: (0, j)),
        ],
        out_specs=pl.BlockSpec((tm, tn), lambda i, j: (i, j)),
        out_shape=jax.ShapeDtypeStruct((m, d), F32),
        scratch_shapes=[pltpu.VMEM((tm, n_attn * wa + wg), F32 if exact else BF16)],
        compiler_params=_params("parallel", "arbitrary"),
        name=name,
    )(*attn_parts, gm, x, g_attn.reshape(1, -1), g_gmlp.reshape(1, -1), w)


ROUTE_LANES = 128


def _router_kernel(h_ref, g_ref, wr_ref, br_ref, c0_ref, m_ref, route_ref, cnt_ref, carry_ref):
    tm = h_ref.shape[0]
    n_exp = wr_ref.shape[1]

    @pl.when(pl.program_id(0) == 0)
    def _():
        carry_ref[...] = c0_ref[...]

    h = h_ref[...]
    m = h * _rms_scale(h) * g_ref[...]
    m_ref[...] = m
    logits = _mm(m, wr_ref[...], exact=True) + br_ref[...]

    lane = lax.broadcasted_iota(jnp.int32, (tm, n_exp), 1)
    work = logits
    vals, sels, idxs = [], [], []
    for _ in range(TOP_K):
        mx = jnp.max(work, axis=-1, keepdims=True)
        idx = jnp.min(jnp.where(work == mx, lane, n_exp), axis=-1, keepdims=True)
        sel = lane == idx
        vals.append(mx)
        idxs.append(idx)
        sels.append(sel)
        work = jnp.where(sel, -jnp.inf, work)
    es = [jnp.exp(v - vals[0]) for v in vals]
    inv = 1.0 / functools.reduce(lambda a, b: a + b, es)

    hot = functools.reduce(lambda a, b: a + b, [s.astype(F32) for s in sels])
    ri = lax.broadcasted_iota(jnp.int32, (tm, tm), 0)
    rj = lax.broadcasted_iota(jnp.int32, (tm, tm), 1)
    earlier = (rj < ri).astype(BF16)
    base = jnp.dot(earlier, hot.astype(BF16), preferred_element_type=F32) + carry_ref[...]
    ranks = [jnp.sum(jnp.where(s, base, 0.0), axis=-1, keepdims=True) for s in sels]
    carry_ref[...] = carry_ref[...] + jnp.sum(hot, axis=0, keepdims=True)
    cnt_ref[...] = carry_ref[...]

    ol = lax.broadcasted_iota(jnp.int32, (tm, ROUTE_LANES), 1)
    out = jnp.zeros((tm, ROUTE_LANES), F32)
    for k in range(TOP_K):
        out = jnp.where(ol == k, idxs[k].astype(F32), out)
        out = jnp.where(ol == TOP_K + k, ranks[k], out)
        out = jnp.where(ol == 2 * TOP_K + k, es[k] * inv, out)
    route_ref[...] = out


def _router(h, g, w_r, b_r, counts0, *, tm, name):
    t, d = h.shape
    n_exp = w_r.shape[1]
    return pl.pallas_call(
        _router_kernel,
        grid=(t // tm,),
        in_specs=[
            pl.BlockSpec((tm, d), lambda i: (i, 0)),
            pl.BlockSpec((1, d), lambda i: (0, 0)),
            pl.BlockSpec((d, n_exp), lambda i: (0, 0)),
            pl.BlockSpec((1, n_exp), lambda i: (0, 0)),
            pl.BlockSpec((1, n_exp), lambda i: (0, 0)),
        ],
        out_specs=[
            pl.BlockSpec((tm, d), lambda i: (i, 0)),
            pl.BlockSpec((tm, ROUTE_LANES), lambda i: (i, 0)),
            pl.BlockSpec((1, n_exp), lambda i: (0, 0)),
        ],
        out_shape=[
            jax.ShapeDtypeStruct((t, d), F32),
            jax.ShapeDtypeStruct((t, ROUTE_LANES), F32),
            jax.ShapeDtypeStruct((1, n_exp), F32),
        ],
        scratch_shapes=[pltpu.VMEM((1, n_exp), F32)],
        compiler_params=_params("arbitrary"),
        name=name,
    )(h, g.reshape(1, d), w_r, b_r.reshape(1, n_exp), counts0)


def _scatter_kernel(*refs, zero_fill, n_exp):
    if zero_fill:
        pstart_ref, pend_ref, dest_ref, m_ref, rows_hbm, zeros_ref, sem = refs
    else:
        pstart_ref, pend_ref, dest_ref, m_ref, _, rows_hbm, zeros_ref, sem = refs
    ts = dest_ref.shape[0] // TOP_K
    i = pl.program_id(0)

    if zero_fill:
        @pl.when(i == 0)
        def _():
            zeros_ref[...] = jnp.zeros_like(zeros_ref)

            def fill(e):
                return pltpu.make_async_copy(
                    zeros_ref, rows_hbm.at[pl.ds(pl.multiple_of(pend_ref[e] - MOE_CHUNK, MOE_CHUNK), MOE_CHUNK)], sem)

            def start(e, carry):
                @pl.when(pend_ref[e] > pstart_ref[e])
                def _():
                    fill(e).start()
                return carry

            def wait(e, carry):
                @pl.when(pend_ref[e] > pstart_ref[e])
                def _():
                    fill(e).wait()
                return carry

            lax.fori_loop(0, n_exp, start, 0)
            lax.fori_loop(0, n_exp, wait, 0)

    def row_copy(src, dst):
        return pltpu.make_async_copy(m_ref.at[pl.ds(src, 1)], rows_hbm.at[pl.ds(dst, 1)], sem)

    def start(t, carry):
        for k in range(TOP_K):
            row_copy(t, dest_ref[TOP_K * t + k]).start(priority=k % 2)
        return carry

    def wait(t, carry):
        for k in range(TOP_K):
            row_copy(0, 0).wait()
        return carry

    unroll = min(ts, 8)
    lax.fori_loop(0, ts, start, 0, unroll=unroll)
    lax.fori_loop(0, ts, wait, 0, unroll=unroll)


def _scatter_rows(m, dest, pstart, pend, rows_in, *, ts, n_rows, name):
    t, d = m.shape
    zero_fill = rows_in is None
    n_exp = pstart.shape[0]
    in_specs = [
        pl.BlockSpec((ts * TOP_K,), lambda i, *_: (i,), memory_space=pltpu.SMEM),
        pl.BlockSpec((ts, d), lambda i, *_: (i, 0)),
    ]
    args = [dest, m]
    aliases = {}
    if not zero_fill:
        in_specs.append(pl.BlockSpec(memory_space=pl.ANY))
        args.append(rows_in)
        aliases = {4: 0}
    return pl.pallas_call(
        functools.partial(_scatter_kernel, zero_fill=zero_fill, n_exp=n_exp),
        grid_spec=pltpu.PrefetchScalarGridSpec(
            num_scalar_prefetch=2,
            grid=(t // ts,),
            in_specs=in_specs,
            out_specs=pl.BlockSpec(memory_space=pl.ANY),
            scratch_shapes=[pltpu.VMEM((MOE_CHUNK, d), F32), pltpu.SemaphoreType.DMA(())],
        ),
        out_shape=jax.ShapeDtypeStruct((n_rows, d), F32),
        input_output_aliases=aliases,
        compiler_params=_params("arbitrary"),
        name=name,
    )(pstart, pend, *args)


def _expert_kernel(ir_ref, inch_ref, wge_ref, wgt_ref, wde_ref, wdt_ref,
                   rows_hbm, wg_ref, wl_ref, wd_ref, bg_ref, bl_ref, bd_ref, ys_hbm,
                   stage, xb, hdn, ybuf, sem_in, sem_out, *, n_ff, n_dt):
    i = pl.program_id(0)
    s = pl.program_id(1)
    n_items = pl.num_programs(0)
    nch = inch_ref[i]
    tile = ybuf.shape[2]

    def chunk(c):
        return slice(c * MOE_CHUNK, (c + 1) * MOE_CHUNK)

    def row_window(item, c):
        return pl.ds(pl.multiple_of(ir_ref[item], MOE_CHUNK) + c * MOE_CHUNK, MOE_CHUNK)

    def load(item, c):
        return pltpu.make_async_copy(rows_hbm.at[row_window(item, c)], stage.at[chunk(c)], sem_in.at[c])

    def store(item, c, n):
        return pltpu.make_async_copy(
            ybuf.at[n, chunk(c)], ys_hbm.at[row_window(item, c), pl.ds(n * tile, tile)], sem_out.at[c])

    def for_chunks(count, fn):
        for c in range(MOE_MAX_CHUNKS):
            pl.when(c < count)(functools.partial(fn, c))

    def for_sizes(fn):
        for nc in range(1, MOE_MAX_CHUNKS + 1):
            pl.when(nch == nc)(functools.partial(fn, nc * MOE_CHUNK))

    @pl.when(s == 0)
    def _():
        @pl.when(i == 0)
        def _():
            for_chunks(nch, lambda c: load(0, c).start())

        def land(c):
            load(i, c).wait()
            xb[chunk(c), :] = stage[chunk(c), :].astype(BF16)

        for_chunks(nch, land)

    @pl.when((s < n_ff) & (nch > 0))
    def _():
        def phase1(rows):
            wg = wg_ref[...].astype(BF16)
            wl = wl_ref[...].astype(BF16)
            x = xb[:rows, :]
            gate = jnp.dot(x, wg, preferred_element_type=F32) + bg_ref[...]
            lin = jnp.dot(x, wl, preferred_element_type=F32) + bl_ref[...]
            gate = jnp.minimum(gate, SWIGLU_LIMIT)
            lin = jnp.clip(lin, -SWIGLU_LIMIT, SWIGLU_LIMIT)
            act = gate * jax.nn.sigmoid(SWIGLU_ALPHA * gate) * (lin + 1.0)
            hdn[s, :rows, :] = act.astype(BF16)

        for_sizes(phase1)

    @pl.when(s == n_ff)
    def _():
        @pl.when(i > 0)
        def _():
            def drain(c):
                for n in range(n_dt):
                    store(i - 1, c, n).wait()

            for_chunks(inch_ref[jnp.maximum(i - 1, 0)], drain)

        @pl.when(i + 1 < n_items)
        def _():
            nxt = jnp.minimum(i + 1, n_items - 1)
            for_chunks(inch_ref[nxt], lambda c: load(nxt, c).start())

    @pl.when((s >= n_ff) & (nch > 0))
    def _():
        n = s - n_ff

        def phase2(rows):
            wd = wd_ref[...].astype(BF16)
            y = bd_ref[...]
            for f in range(n_ff):
                y = y + jnp.dot(hdn[f, :rows, :], wd[f * tile:(f + 1) * tile, :],
                                preferred_element_type=F32)
            ybuf[n, :rows, :] = y

        for_sizes(phase2)

    @pl.when(s == n_ff + n_dt - 1)
    def _():
        def flush(c):
            for n in range(n_dt):
                store(i, c, n).start()

        for_chunks(nch, flush)

        @pl.when(i == n_items - 1)
        def _():
            def drain(c):
                for n in range(n_dt):
                    store(i, c, n).wait()

            for_chunks(nch, drain)


def _expert_ffn(rows, w_gu, b_gu, w_down, b_down, item_e, item_row0, item_nch):
    n_rows, d = rows.shape
    n_exp, _, two_ff = w_gu.shape
    d_ff = two_ff // 2
    tile = MOE_FF_TILE
    n_ff = d_ff // tile
    n_dt = d // tile
    n_steps = n_ff + n_dt
    n_items = item_e.shape[0]
    r_max = MOE_MAX_CHUNKS * MOE_CHUNK

    valid = (item_nch > 0)[:, None]
    last = jnp.sum(item_nch > 0) - 1
    e_last = item_e[last]
    e_next = item_e[jnp.minimum(jnp.arange(n_items) + 1, last)][:, None]
    step = jnp.arange(n_steps, dtype=jnp.int32)[None, :]
    first = step < n_ff
    e_col = item_e[:, None]
    wg_e = jnp.where(valid, jnp.where(first, e_col, e_next), e_last).reshape(-1)
    wg_t = jnp.where(valid & first, step, 0).reshape(-1)
    wd_e = jnp.broadcast_to(jnp.where(valid, e_col, e_last), (n_items, n_steps)).reshape(-1)
    wd_t = jnp.where(valid, jnp.where(first, 0, step - n_ff), n_dt - 1).reshape(-1)

    def at(i, s):
        return i * n_steps + s

    def gate_map(i, s, ir, inch, wge, wgt, wde, wdt):
        return wge[at(i, s)], 0, wgt[at(i, s)]

    def lin_map(i, s, ir, inch, wge, wgt, wde, wdt):
        return wge[at(i, s)], 0, n_ff + wgt[at(i, s)]

    def down_map(i, s, ir, inch, wge, wgt, wde, wdt):
        return wde[at(i, s)], 0, wdt[at(i, s)]

    return pl.pallas_call(
        functools.partial(_expert_kernel, n_ff=n_ff, n_dt=n_dt),
        grid_spec=pltpu.PrefetchScalarGridSpec(
            num_scalar_prefetch=6,
            grid=(n_items, n_steps),
            in_specs=[
                pl.BlockSpec(memory_space=pl.ANY),
                pl.BlockSpec((None, d, tile), gate_map),
                pl.BlockSpec((None, d, tile), lin_map),
                pl.BlockSpec((None, d_ff, tile), down_map),
                pl.BlockSpec((None, 1, tile), gate_map),
                pl.BlockSpec((None, 1, tile), lin_map),
                pl.BlockSpec((None, 1, tile), down_map),
            ],
            out_specs=pl.BlockSpec(memory_space=pl.ANY),
            scratch_shapes=[
                pltpu.VMEM((r_max, d), F32),
                pltpu.VMEM((r_max, d), BF16),
                pltpu.VMEM((n_ff, r_max, tile), BF16),
                pltpu.VMEM((n_dt, r_max, tile), F32),
                pltpu.SemaphoreType.DMA((MOE_MAX_CHUNKS,)),
                pltpu.SemaphoreType.DMA((MOE_MAX_CHUNKS,)),
            ],
        ),
        out_shape=jax.ShapeDtypeStruct((n_rows, d), F32),
        compiler_params=_params("arbitrary", "arbitrary"),
        name="expert_ffn",
    )(item_row0, item_nch, wg_e.astype(jnp.int32), wg_t.astype(jnp.int32),
      wd_e.astype(jnp.int32), wd_t.astype(jnp.int32), rows, w_gu, w_gu, w_down,
      b_gu.reshape(n_exp, 1, two_ff), b_gu.reshape(n_exp, 1, two_ff),
      b_down.reshape(n_exp, 1, d))


def _combine_kernel(dest_ref, route_ref, h_ref, ys_hbm, o_ref, buf, sem):
    tg = h_ref.shape[0]

    def row_copy(src, k, t):
        return pltpu.make_async_copy(ys_hbm.at[pl.ds(src, 1)], buf.at[k, pl.ds(t, 1)], sem)

    def start(t, carry):
        for k in range(TOP_K):
            row_copy(dest_ref[TOP_K * t + k], k, t).start(priority=k % 2)
        return carry

    def wait(t, carry):
        for k in range(TOP_K):
            row_copy(0, k, 0).wait()
        return carry

    unroll = min(tg, 8)
    lax.fori_loop(0, tg, start, 0, unroll=unroll)
    lax.fori_loop(0, tg, wait, 0, unroll=unroll)
    out = h_ref[...]
    for k in range(TOP_K):
        out = out + route_ref[:, 2 * TOP_K + k:2 * TOP_K + k + 1] * buf[k]
    o_ref[...] = out


def _combine(dest, route, h, ys, *, tg, name):
    t, d = h.shape
    return pl.pallas_call(
        _combine_kernel,
        grid=(t // tg,),
        in_specs=[
            pl.BlockSpec((tg * TOP_K,), lambda i: (i,), memory_space=pltpu.SMEM),
            pl.BlockSpec((tg, ROUTE_LANES), lambda i: (i, 0)),
            pl.BlockSpec((tg, d), lambda i: (i, 0)),
            pl.BlockSpec(memory_space=pl.ANY),
        ],
        out_specs=pl.BlockSpec((tg, d), lambda i: (i, 0)),
        out_shape=jax.ShapeDtypeStruct((t, d), F32),
        scratch_shapes=[pltpu.VMEM((TOP_K, tg, d), F32), pltpu.SemaphoreType.DMA(())],
        compiler_params=_params("arbitrary"),
        name=name,
    )(dest, route, h, ys)


def _ple_kernel(h_ref, p_ref, g_ref, wg_ref, wp_ref, o_ref, hn_ref):
    rows = min(h_ref.shape[0], 128)

    def body(c, carry):
        sl = pl.ds(pl.multiple_of(c * rows, rows), rows)
        h = h_ref[sl, :]
        hn_ref[sl, :] = (h * _rms_scale(h) * g_ref[...]).astype(BF16)
        return carry

    lax.fori_loop(0, h_ref.shape[0] // rows, body, 0)
    gate = jax.nn.sigmoid(jnp.dot(hn_ref[...], wg_ref[...].astype(BF16), preferred_element_type=F32))
    proj = jnp.dot(p_ref[...].astype(BF16), wp_ref[...].astype(BF16), preferred_element_type=F32)
    o_ref[...] = h_ref[...] + gate * proj


def _ple(h, p, g, w_gate, w_proj, *, tm, name):
    t, d = h.shape
    pd = p.shape[1]
    return pl.pallas_call(
        _ple_kernel,
        grid=(t // tm,),
        in_specs=[
            pl.BlockSpec((tm, d), lambda i: (i, 0)),
            pl.BlockSpec((tm, pd), lambda i: (i, 0)),
            pl.BlockSpec((1, d), lambda i: (0, 0)),
            pl.BlockSpec((d, d), lambda i: (0, 0)),
            pl.BlockSpec((pd, d), lambda i: (0, 0)),
        ],
        out_specs=pl.BlockSpec((tm, d), lambda i: (i, 0)),
        out_shape=jax.ShapeDtypeStruct((t, d), F32),
        scratch_shapes=[pltpu.VMEM((tm, d), BF16)],
        compiler_params=_params("parallel"),
        name=name,
    )(h, p, g.reshape(1, d), w_gate, w_proj)


def _decode_kernel(*refs, groups, hpg):
    ng = len(groups)
    z_ref = refs[0]
    caches = refs[1:1 + ng]
    (cos_ref, sa_ref, sb_ref, gq_ref, gk_ref, lng_ref, lnb_ref, ws0_ref, bs0_ref,
     attn_ref, kn_ref, gm_ref, vn_ref) = refs[1 + ng:]
    heads = ng * hpg
    aw = heads * HEAD_DIM
    scale = HEAD_DIM ** -0.5
    tabs = (cos_ref[...], sa_ref[...], sb_ref[...])

    def head(base, hh):
        return z_ref[:, base + hh * HEAD_DIM:base + (hh + 1) * HEAD_DIM]

    outs, lses = [], []
    for g in range(ng):
        c_ref = caches[g]
        for j in range(hpg):
            hh = g * hpg + j
            qn = _norm_rope(head(0, hh), gq_ref[...], *tabs) * scale
            kn = _norm_rope(head(aw, hh), gk_ref[...], *tabs)
            vv = head(2 * aw, hh)
            kn_ref[:, hh * HEAD_DIM:(hh + 1) * HEAD_DIM] = kn
            kc = c_ref[:, 0, j, :]
            vc = c_ref[:, 1, j, :]
            s_c = jnp.sum(kc * qn, axis=-1, keepdims=True)
            s_n = jnp.sum(kn * qn, axis=-1, keepdims=True)
            m = jnp.maximum(jnp.max(s_c, axis=0, keepdims=True), s_n)
            p_c = jnp.exp(s_c - m)
            p_n = jnp.exp(s_n - m)
            den = jnp.sum(p_c, axis=0, keepdims=True) + p_n
            o = (jnp.sum(p_c * vc, axis=0, keepdims=True) + p_n * vv) / den
            outs.append(o)
            lses.append(m + jnp.log(den))

    for j in range(hpg):
        ls = [lses[g * hpg + j] for g in range(ng)]
        mx = functools.reduce(jnp.maximum, ls)
        es = [jnp.exp(l - mx) for l in ls]
        inv = 1.0 / functools.reduce(lambda a, b: a + b, es)
        for g in range(ng):
            hh = g * hpg + j
            attn_ref[:, hh * HEAD_DIM:(hh + 1) * HEAD_DIM] = outs[hh] * (es[g] * inv)

    gw = gm_ref.shape[1]
    for g in range(gw // GMLP_CH):
        cols = slice(g * GMLP_CH, (g + 1) * GMLP_CH)
        v = _gelu(z_ref[:, 3 * aw + gw + g * GMLP_CH:3 * aw + gw + (g + 1) * GMLP_CH])
        mu = jnp.mean(v, axis=-1, keepdims=True)
        vc = v - mu
        var = jnp.mean(vc * vc, axis=-1, keepdims=True)
        vn = vc * lax.rsqrt(var + EPS) * lng_ref[:, cols] + lnb_ref[:, cols]
        vn_ref[:, cols] = vn
        u = _gelu(z_ref[:, 3 * aw + g * GMLP_CH:3 * aw + (g + 1) * GMLP_CH])
        gm_ref[:, cols] = u * (ws0_ref[:, cols] * vn + bs0_ref[:, cols])


def _sample_mixers(zs, caches, tables, g_q, g_k, ln_g, ln_b, w_s, b_s, *, hpg):
    bsz, in_width = zs.shape
    ng = len(DILATION_GROUPS)
    aw = ng * hpg * HEAD_DIM
    gw = (in_width - 3 * aw) // 2
    in_specs = [pl.BlockSpec((None, 1, in_width), lambda b: (b, 0, 0))]
    cache_args = []
    for (window, dil), c in zip(DILATION_GROUPS, caches):
        wc = c.shape[-4]
        assert wc == window and c.shape[-5] == bsz, "window caches must hold a full window"
        steps = wc // dil
        cache_args.append(c.reshape(bsz, steps, dil, 2, hpg, HEAD_DIM))
        in_specs.append(pl.BlockSpec((None, steps, None, 2, hpg, HEAD_DIM),
                                     lambda b: (b, 0, 0, 0, 0, 0)))
    vec = pl.BlockSpec((1, HEAD_DIM), lambda b: (0, 0))
    wide = pl.BlockSpec((1, gw), lambda b: (0, 0))
    in_specs += [vec] * 5 + [wide] * 4

    def out(w):
        return pl.BlockSpec((None, 1, w), lambda b: (b, 0, 0)), jax.ShapeDtypeStruct((bsz, 1, w), F32)

    specs, shapes = zip(out(aw), out(aw), out(gw), out(gw))
    groups = w_s.shape[0]
    ws0 = jnp.repeat(w_s[:, 0, 0], GMLP_CH).reshape(1, gw)
    bs0 = jnp.repeat(b_s[:, 0], GMLP_CH).reshape(1, gw)
    assert groups * GMLP_CH == gw
    res = pl.pallas_call(
        functools.partial(_decode_kernel, groups=DILATION_GROUPS, hpg=hpg),
        grid=(bsz,),
        in_specs=in_specs,
        out_specs=list(specs),
        out_shape=list(shapes),
        compiler_params=_params("parallel"),
        name="sample_mixers",
    )(zs.reshape(bsz, 1, in_width), *cache_args, *tables,
      g_q.reshape(1, HEAD_DIM), g_k.reshape(1, HEAD_DIM),
      ln_g.reshape(1, gw), ln_b.reshape(1, gw), ws0, bs0)
    return [r.reshape(bsz, -1) for r in res]


def _expert_layout(counts, n_items):
    counts = counts.astype(jnp.int32)
    padded = (counts + MOE_CHUNK - 1) // MOE_CHUNK * MOE_CHUNK
    pend = jnp.cumsum(padded)
    pstart = pend - padded
    r_max = MOE_MAX_CHUNKS * MOE_CHUNK
    per_e = (padded + r_max - 1) // r_max
    item_end = jnp.cumsum(per_e)
    total = item_end[-1]
    ids = jnp.arange(n_items, dtype=jnp.int32)
    clamped = jnp.minimum(ids, total - 1)
    e = jnp.sum((item_end[None, :] <= clamped[:, None]).astype(jnp.int32), axis=1)
    local = clamped - (item_end[e] - per_e[e])
    row0 = pstart[e] + local * r_max
    nch = jnp.minimum(padded[e] - local * r_max, r_max) // MOE_CHUNK
    nch = jnp.where(ids < total, nch, 0)
    return pstart, pend, e, row0.astype(jnp.int32), nch.astype(jnp.int32)


def _destinations(route, pstart):
    e = route[:, :TOP_K].astype(jnp.int32)
    rank = route[:, TOP_K:2 * TOP_K].astype(jnp.int32)
    return (pstart[e] + rank).reshape(-1)


def kernel(x_prompt, x_sample, cache_kv_w128, cache_kv_w512, cache_kv_w2048, p_prompt, p_sample,
           g_mix, w_in, g_q, g_k, gmlp_ln_g, gmlp_ln_b, gmlp_w_s, gmlp_b_s, g_out_attn, g_out_gmlp,
           w_out, g_ffn, w_router, b_router, w_gu, b_gu, w_down, b_down, g_ple, w_ple_gate,
           w_ple_proj):
    depth = g_mix.shape[0]
    assert depth == 1, "single-layer step"
    batch, seq, d = x_prompt.shape
    dec_batch, dec_seq, _ = x_sample.shape
    assert dec_seq == 1
    in_width = w_in.shape[2]
    gw = gmlp_ln_g.shape[1] * gmlp_ln_g.shape[2]
    aw = (in_width - 2 * gw) // 3
    ng = len(DILATION_GROUPS)
    hpg = aw // HEAD_DIM // ng
    n_exp = w_router.shape[2]
    tp = batch * seq
    ts = dec_batch * dec_seq

    def layer(a):
        return a.reshape(a.shape[1:])

    caches = (cache_kv_w128, cache_kv_w512, cache_kv_w2048)
    (p_prompt, p_sample, g_mix, w_in, g_q, g_k, gmlp_ln_g, gmlp_ln_b, gmlp_w_s, gmlp_b_s,
     g_out_attn, g_out_gmlp, w_out, g_ffn, w_router, b_router, w_gu, b_gu, w_down, b_down, g_ple,
     w_ple_gate, w_ple_proj) = [layer(a) for a in (
         p_prompt, p_sample, g_mix, w_in, g_q, g_k, gmlp_ln_g, gmlp_ln_b, gmlp_w_s, gmlp_b_s,
         g_out_attn, g_out_gmlp, w_out, g_ffn, w_router, b_router, w_gu, b_gu, w_down, b_down,
         g_ple, w_ple_gate, w_ple_proj)]

    w_in_b, w_out_b = w_in.astype(BF16), w_out.astype(BF16)
    w_ple_gate_b, w_ple_proj_b = w_ple_gate.astype(BF16), w_ple_proj.astype(BF16)
    xp = x_prompt.reshape(tp, d)
    z = _norm_matmul(xp, g_mix, w_in_b, tm=512, tn=in_width // 2, name="prompt_in_proj",
                     rows_inner=True)
    z3 = z.reshape(batch, seq, in_width)
    tables_p = _rope_tables(jnp.arange(seq, dtype=jnp.int32))
    attn_p, kn_p = _prompt_attention(z3, g_q, g_k, tables_p, batch=batch, seq=seq, hpg=hpg)
    gm_p = _prompt_gmlp(z3, gmlp_ln_g, gmlp_ln_b, gmlp_w_s, gmlp_b_s,
                        batch=batch, seq=seq, width=gw, col_block=3 * aw // gw)
    hp = _out_proj([a.reshape(tp, -1) for a in attn_p], gm_p.reshape(tp, gw), xp,
                   g_out_attn, g_out_gmlp, w_out_b, tm=512, tn=d, name="prompt_out_proj")

    xs = x_sample.reshape(ts, d)
    zs = _norm_matmul(xs, g_mix, w_in, tm=ts, tn=512, name="sample_in_proj", exact=True)
    tables_s = _rope_tables(PAST_LEN + jnp.arange(dec_seq, dtype=jnp.int32))
    attn_s, kn_s, gm_s, vn_s = _sample_mixers(
        zs, caches, tables_s, g_q, g_k, gmlp_ln_g, gmlp_ln_b, gmlp_w_s, gmlp_b_s, hpg=hpg)
    gwid = hpg * HEAD_DIM
    hs = _out_proj([attn_s[:, g * gwid:(g + 1) * gwid] for g in range(ng)], gm_s, xs,
                   g_out_attn, g_out_gmlp, w_out, tm=ts, tn=512, name="sample_out_proj",
                   exact=True)

    zero_counts = jnp.zeros((1, n_exp), F32)
    m_p, route_p, counts_p = _router(hp, g_ffn, w_router, b_router, zero_counts,
                                     tm=512, name="prompt_router")
    m_s, route_s, counts = _router(hs, g_ffn, w_router, b_router, counts_p,
                                   tm=ts, name="sample_router")
    n_assign = (tp + ts) * TOP_K
    n_chunks = -(-n_assign // MOE_CHUNK) + n_exp
    n_rows = n_chunks * MOE_CHUNK
    n_items = n_exp + -(-n_chunks // MOE_MAX_CHUNKS)
    pstart, pend, item_e, item_row0, item_nch = _expert_layout(counts.reshape(n_exp), n_items)
    dest_p = _destinations(route_p, pstart)
    dest_s = _destinations(route_s, pstart)
    rows = _scatter_rows(m_p, dest_p, pstart, pend, None, ts=512, n_rows=n_rows,
                         name="prompt_scatter")
    rows = _scatter_rows(m_s, dest_s, pstart, pend, rows, ts=ts, n_rows=n_rows,
                         name="sample_scatter")
    ys = _expert_ffn(rows, w_gu, b_gu, w_down, b_down, item_e, item_row0, item_nch)
    hp = _combine(dest_p, route_p, hp, ys, tg=256, name="prompt_combine")
    hs = _combine(dest_s, route_s, hs, ys, tg=ts, name="sample_combine")

    yp = _ple(hp, p_prompt.reshape(tp, -1), g_ple, w_ple_gate_b, w_ple_proj_b,
              tm=512, name="prompt_ple")
    ysm = _ple(hs, p_sample.reshape(ts, -1), g_ple, w_ple_gate_b, w_ple_proj_b,
               tm=ts, name="sample_ple")

    v0 = 2 * aw
    kv_prompt, kv_sample = [], []
    for g, (window, _) in enumerate(DILATION_GROUPS):
        wc = min(window, seq)
        kg = kn_p[g][:, seq - wc:].reshape(batch, wc, hpg, HEAD_DIM)
        vg = z3[:, seq - wc:, v0 + g * gwid:v0 + (g + 1) * gwid].reshape(batch, wc, hpg, HEAD_DIM)
        kv_prompt.append(jnp.stack([kg, vg], axis=2)[None])
        ksg = kn_s[:, g * gwid:(g + 1) * gwid].reshape(dec_batch, dec_seq, hpg, HEAD_DIM)
        vsg = zs[:, v0 + g * gwid:v0 + (g + 1) * gwid].reshape(dec_batch, dec_seq, hpg, HEAD_DIM)
        kv_sample.append(jnp.stack([ksg, vsg], axis=2)[None])
    gmlp_v = vn_s.reshape(1, dec_batch, dec_seq, gw // GMLP_CH, GMLP_CH)
    return (yp.reshape(batch, seq, d), ysm.reshape(dec_batch, dec_seq, d),
            *kv_prompt, *kv_sample, gmlp_v)
```

```python
import functools

import jax
import jax.numpy as jnp
from jax import lax
from jax.experimental import pallas as pl
from jax.experimental.pallas import tpu as pltpu

F32 = jnp.float32
BF16 = jnp.bfloat16

EPS = 1e-6
HEAD_DIM = 128
ROT_DIM = HEAD_DIM // 4
ROPE_THETA = 500000.0
DILATION_GROUPS = ((128, 1), (512, 4), (2048, 16))
PAST_LEN = 16384
QBLK = 128
CHUNK = 128
GMLP_CH = 128
TOP_K = 4
SWIGLU_LIMIT = 7.0
SWIGLU_ALPHA = 1.702

V7X_VMEM_BYTES = 64 * 1024 * 1024
VMEM_LIMIT = V7X_VMEM_BYTES - 8 * 1024 * 1024

MOE_CHUNK = 256
MOE_MAX_CHUNKS = 5
MOE_FF_TILE = 256
WEIGHT_RING = 3


def _params(*sem):
    return pltpu.CompilerParams(dimension_semantics=sem, vmem_limit_bytes=VMEM_LIMIT)


def _split_bf16(a):
    hi = a.astype(BF16)
    lo = (a - hi.astype(F32)).astype(BF16)
    return hi, lo


def _mm(a, b, exact=False):
    d = functools.partial(jnp.dot, preferred_element_type=F32)
    if not exact:
        return d(a.astype(BF16), b.astype(BF16))
    ah, al = _split_bf16(a)
    bh, bl = _split_bf16(b)
    return d(ah, bh) + (d(ah, bl) + d(al, bh))


def _gelu(x):
    return 0.5 * x * (1.0 + jnp.tanh(0.7978845608028654 * (x + 0.044715 * (x * x * x))))


def _rms_scale(x):
    return lax.rsqrt(jnp.mean(x * x, axis=-1, keepdims=True) + EPS)


def _norm_matmul_kernel(x_ref, g_ref, w_ref, o_ref, xn_ref, *, exact, rows_inner):
    def normalise():
        rows = min(x_ref.shape[0], 128)

        def body(c, carry):
            sl = pl.ds(pl.multiple_of(c * rows, rows), rows)
            x = x_ref[sl, :]
            xn_ref[sl, :] = (x * _rms_scale(x) * g_ref[...]).astype(xn_ref.dtype)
            return carry

        lax.fori_loop(0, x_ref.shape[0] // rows, body, 0)

    if rows_inner:
        normalise()
    else:
        pl.when(pl.program_id(1) == 0)(normalise)

    o_ref[...] = _mm(xn_ref[...], w_ref[...], exact)


def _norm_matmul(x, g, w, *, tm, tn, name, exact=False, rows_inner=False):
    m, k = x.shape
    n = w.shape[1]
    if rows_inner:
        grid = (n // tn, m // tm)
        row, col = (lambda j, i: i), (lambda j, i: j)
    else:
        grid = (m // tm, n // tn)
        row, col = (lambda i, j: i), (lambda i, j: j)
    return pl.pallas_call(
        functools.partial(_norm_matmul_kernel, exact=exact, rows_inner=rows_inner),
        grid=grid,
        in_specs=[
            pl.BlockSpec((tm, k), lambda a, b: (row(a, b), 0)),
            pl.BlockSpec((1, k), lambda a, b: (0, 0)),
            pl.BlockSpec((k, tn), lambda a, b: (0, col(a, b))),
        ],
        out_specs=pl.BlockSpec((tm, tn), lambda a, b: (row(a, b), col(a, b))),
        out_shape=jax.ShapeDtypeStruct((m, n), F32),
        scratch_shapes=[pltpu.VMEM((tm, k), F32 if exact else BF16)],
        compiler_params=_params("parallel", "arbitrary"),
        name=name,
    )(x, g.reshape(1, k), w)


def _rope_tables(pos):
    half = ROT_DIM // 2
    inv = 1.0 / (ROPE_THETA ** (jnp.arange(half, dtype=F32) * (2.0 / ROT_DIM)))
    ang = pos.astype(F32)[:, None] * inv[None, :]
    cos, sin = jnp.cos(ang), jnp.sin(ang)
    t = pos.shape[0]
    pad = HEAD_DIM - ROT_DIM
    cos_t = jnp.concatenate([cos, cos, jnp.ones((t, pad), F32)], axis=1)
    sa = jnp.concatenate([-sin, jnp.zeros((t, HEAD_DIM - half), F32)], axis=1)
    sb = jnp.concatenate([jnp.zeros((t, half), F32), sin, jnp.zeros((t, pad), F32)], axis=1)
    return cos_t, sa, sb


def _norm_rope(x, gain, cos_t, sa, sb):
    half = ROT_DIM // 2
    xn = x * _rms_scale(x) * gain
    up = pltpu.roll(xn, HEAD_DIM - half, 1)
    dn = pltpu.roll(xn, half, 1)
    return xn * cos_t + up * sa + dn * sb


def _band_block(q, k, v, mask):
    s = lax.dot_general(q.astype(BF16), k.astype(BF16), (((1,), (1,)), ((), ())),
                        preferred_element_type=F32)
    s = jnp.where(mask, s, -jnp.inf)
    m = jnp.max(s, axis=-1, keepdims=True)
    p = jnp.exp(s - m)
    den = jnp.sum(p, axis=-1, keepdims=True)
    o = jnp.dot((p / den).astype(BF16), v.astype(BF16), preferred_element_type=F32)
    return o, m + jnp.log(den)


def _attn_kernel(*refs, seq, groups):
    ng = len(groups)
    qkv = refs[:3 * ng]
    cos_ref, sa_ref, sb_ref, gq_ref, gk_ref = refs[3 * ng:3 * ng + 5]
    outs = refs[3 * ng + 5:3 * ng + 5 + 2 * ng]
    o_refs, kn_refs = outs[:ng], outs[ng:]
    qs, ks = refs[3 * ng + 5 + 2 * ng:3 * ng + 7 + 2 * ng]
    ls_refs = refs[3 * ng + 7 + 2 * ng:]
    scale = HEAD_DIM ** -0.5
    nrm_rows = 256

    qi = lax.broadcasted_iota(jnp.int32, (QBLK, 2 * QBLK), 0)
    kj = lax.broadcasted_iota(jnp.int32, (QBLK, 2 * QBLK), 1)
    qi1 = lax.broadcasted_iota(jnp.int32, (QBLK, QBLK), 0)
    kj1 = lax.broadcasted_iota(jnp.int32, (QBLK, QBLK), 1)

    for g, (window, dil) in enumerate(groups):
        q_ref, k_ref, v_ref = qkv[3 * g:3 * g + 3]
        o_ref, kn_ref, ls_ref = o_refs[g], kn_refs[g], ls_refs[g]
        n_steps = window // dil
        nb = seq // dil // QBLK

        def prep(c, carry, q_ref=q_ref, k_ref=k_ref, kn_ref=kn_ref):
            sl = pl.ds(pl.multiple_of(c * nrm_rows, nrm_rows), nrm_rows)
            tabs = (cos_ref[sl, :], sa_ref[sl, :], sb_ref[sl, :])
            qs[sl, :] = _norm_rope(q_ref[sl, :], gq_ref[...], *tabs) * scale
            kn = _norm_rope(k_ref[sl, :], gk_ref[...], *tabs)
            kn_ref[sl, :] = kn
            ks[sl, :] = kn
            return carry

        lax.fori_loop(0, seq // nrm_rows, prep, 0)

        def rows(start, size, dil=dil):
            return pl.ds(start, size) if dil == 1 else pl.ds(start, size, stride=dil)

        band_mask = (kj >= qi + (QBLK - n_steps)) & (kj <= qi + QBLK)
        first_mask = (kj1 <= qi1) & (qi1 - kj1 <= n_steps)

        def one_class(r, carry, v_ref=v_ref, o_ref=o_ref, ls_ref=ls_ref, rows=rows,
                      band_mask=band_mask, first_mask=first_mask, dil=dil, nb=nb):
            sl0 = rows(r, QBLK)
            o, lse = _band_block(qs[sl0, :], ks[sl0, :], v_ref[sl0, :], first_mask)
            o_ref[sl0, :] = o
            ls_ref[sl0, :] = jnp.broadcast_to(lse, (QBLK, HEAD_DIM))

            def one_block(lb, carry2):
                q0 = r + dil * QBLK * lb
                k0 = q0 - dil * QBLK
                slq, slk = rows(q0, QBLK), rows(k0, 2 * QBLK)
                o, lse = _band_block(qs[slq, :], ks[slk, :], v_ref[slk, :], band_mask)
                o_ref[slq, :] = o
                ls_ref[slq, :] = jnp.broadcast_to(lse, (QBLK, HEAD_DIM))
                return carry2

            lax.fori_loop(1, nb, one_block, 0, unroll=min(5, max(nb - 1, 1)))
            return carry

        lax.fori_loop(0, dil, one_class, 0, unroll=min(8, dil) if nb == 1 else 1)

    def merge(c, carry):
        sl = pl.ds(pl.multiple_of(c * nrm_rows, nrm_rows), nrm_rows)
        ls = [ls_refs[g][sl, :] for g in range(ng)]
        mx = functools.reduce(jnp.maximum, ls)
        es = [jnp.exp(l - mx) for l in ls]
        inv = 1.0 / functools.reduce(lambda a, b: a + b, es)
        for g in range(ng):
            o_refs[g][sl, :] = o_refs[g][sl, :] * (es[g] * inv)
        return carry

    lax.fori_loop(0, seq // nrm_rows, merge, 0)


def _prompt_attention(z, g_q, g_k, tables, *, batch, seq, hpg):
    ng = len(DILATION_GROUPS)
    heads = ng * hpg
    blk = (None, seq, HEAD_DIM)

    def col(c):
        return pl.BlockSpec(blk, lambda b, h, c=c: (b, 0, c + h))

    in_specs = []
    for g in range(ng):
        in_specs += [col(g * hpg), col(heads + g * hpg), col(2 * heads + g * hpg)]
    tab = pl.BlockSpec((seq, HEAD_DIM), lambda b, h: (0, 0))
    vec = pl.BlockSpec((1, HEAD_DIM), lambda b, h: (0, 0))
    in_specs += [tab, tab, tab, vec, vec]
    out_spec = pl.BlockSpec(blk, lambda b, h: (b, 0, h))
    out_sds = jax.ShapeDtypeStruct((batch, seq, hpg * HEAD_DIM), F32)
    res = pl.pallas_call(
        functools.partial(_attn_kernel, seq=seq, groups=DILATION_GROUPS),
        grid=(batch, hpg),
        in_specs=in_specs,
        out_specs=[out_spec] * (2 * ng),
        out_shape=[out_sds] * (2 * ng),
        scratch_shapes=[pltpu.VMEM((seq, HEAD_DIM), F32)] * (2 + ng),
        compiler_params=_params("parallel", "arbitrary"),
        name="prompt_attention",
    )(*([z] * (3 * ng)), *tables, g_q.reshape(1, HEAD_DIM), g_k.reshape(1, HEAD_DIM))
    return res[:ng], res[ng:]


def _gmlp_kernel(u_ref, vg_ref, lng_ref, lnb_ref, ws_ref, bs_ref, o_ref):
    groups = ws_ref.shape[0]
    ci = lax.broadcasted_iota(jnp.int32, (CHUNK, CHUNK), 0)
    cj = lax.broadcasted_iota(jnp.int32, (CHUNK, CHUNK), 1)
    for g in range(groups):
        cols = slice(g * GMLP_CH, (g + 1) * GMLP_CH)
        ws = jnp.where(cj <= ci, ws_ref[g], 0.0).astype(BF16)
        bias = bs_ref[:, g:g + 1]
        for n in range(u_ref.shape[0] // CHUNK):
            rows = slice(n * CHUNK, (n + 1) * CHUNK)
            v = _gelu(vg_ref[rows, cols])
            mu = jnp.mean(v, axis=-1, keepdims=True)
            vc = v - mu
            var = jnp.mean(vc * vc, axis=-1, keepdims=True)
            vn = vc * lax.rsqrt(var + EPS) * lng_ref[:, cols] + lnb_ref[:, cols]
            mixed = jnp.dot(ws, vn.astype(BF16), preferred_element_type=F32) + bias
            o_ref[rows, cols] = _gelu(u_ref[rows, cols]) * mixed


def _prompt_gmlp(z, ln_g, ln_b, w_s, b_s, *, batch, seq, width, col_block):
    tc = 4 * CHUNK
    groups = w_s.shape[0]
    return pl.pallas_call(
        _gmlp_kernel,
        grid=(batch, seq // tc),
        in_specs=[
            pl.BlockSpec((None, tc, width), lambda b, n: (b, n, col_block)),
            pl.BlockSpec((None, tc, width), lambda b, n: (b, n, col_block + 1)),
            pl.BlockSpec((1, width), lambda b, n: (0, 0)),
            pl.BlockSpec((1, width), lambda b, n: (0, 0)),
            pl.BlockSpec((groups, CHUNK, CHUNK), lambda b, n: (0, 0, 0)),
            pl.BlockSpec((CHUNK, groups), lambda b, n: (0, 0)),
        ],
        out_specs=pl.BlockSpec((None, tc, width), lambda b, n: (b, n, 0)),
        out_shape=jax.ShapeDtypeStruct((batch, seq, width), F32),
        compiler_params=_params("parallel", "parallel"),
        name="prompt_gmlp",
    )(z, z, ln_g.reshape(1, width), ln_b.reshape(1, width), w_s, b_s.T)


def _out_proj_kernel(*refs, n_attn, exact):
    a_refs = refs[:n_attn]
    gm_ref, x_ref, ga_ref, gg_ref, w_ref, o_ref, yn_ref = refs[n_attn:]
    wa = a_refs[0].shape[1]

    @pl.when(pl.program_id(1) == 0)
    def _():
        rows = min(x_ref.shape[0], 128)

        def body(c, carry):
            sl = pl.ds(pl.multiple_of(c * rows, rows), rows)
            parts = [a[sl, :] for a in a_refs]
            ssq = functools.reduce(lambda p, q: p + q,
                                   [jnp.sum(p * p, axis=-1, keepdims=True) for p in parts])
            rs = lax.rsqrt(ssq / (n_attn * wa) + EPS)
            for idx, p in enumerate(parts):
                cs = slice(idx * wa, (idx + 1) * wa)
                yn_ref[sl, cs] = (p * rs * ga_ref[:, cs]).astype(yn_ref.dtype)
            gm = gm_ref[sl, :]
            yn_ref[sl, n_attn * wa:] = (gm * _rms_scale(gm) * gg_ref[...]).astype(yn_ref.dtype)
            return carry

        lax.fori_loop(0, x_ref.shape[0] // rows, body, 0)

    o_ref[...] = x_ref[...] + _mm(yn_ref[...], w_ref[...], exact)


def _out_proj(attn_parts, gm, x, g_attn, g_gmlp, w, *, tm, tn, name, exact=False):
    m, d = x.shape
    n_attn = len(attn_parts)
    wa = attn_parts[0].shape[1]
    wg = gm.shape[1]
    part = pl.BlockSpec((tm, wa), lambda i, j: (i, 0))
    return pl.pallas_call(
        functools.partial(_out_proj_kernel, n_attn=n_attn, exact=exact),
        grid=(m // tm, d // tn),
        in_specs=[part] * n_attn + [
            pl.BlockSpec((tm, wg), lambda i, j: (i, 0)),
            pl.BlockSpec((tm, tn), lambda i, j: (i, j)),
            pl.BlockSpec((1, n_attn * wa), lambda i, j: (0, 0)),
            pl.BlockSpec((1, wg), lambda i, j: (0, 0)),
            pl.BlockSpec((n_attn * wa + wg, tn), lambda i, j: (0, j)),
        ],
        out_specs=pl.BlockSpec((tm, tn), lambda i, j: (i, j)),
        out_shape=jax.ShapeDtypeStruct((m, d), F32),
        scratch_shapes=[pltpu.VMEM((tm, n_attn * wa + wg), F32 if exact else BF16)],
        compiler_params=_params("parallel", "arbitrary"),
        name=name,
    )(*attn_parts, gm, x, g_attn.reshape(1, -1), g_gmlp.reshape(1, -1), w)


ROUTE_LANES = 128


def _router_kernel(h_ref, g_ref, wr_ref, br_ref, c0_ref, m_ref, route_ref, cnt_ref, carry_ref):
    tm = h_ref.shape[0]
    n_exp = wr_ref.shape[1]

    @pl.when(pl.program_id(0) == 0)
    def _():
        carry_ref[...] = c0_ref[...]

    h = h_ref[...]
    m = h * _rms_scale(h) * g_ref[...]
    m_ref[...] = m
    logits = _mm(m, wr_ref[...], exact=True) + br_ref[...]

    lane = lax.broadcasted_iota(jnp.int32, (tm, n_exp), 1)
    work = logits
    vals, sels, idxs = [], [], []
    for _ in range(TOP_K):
        mx = jnp.max(work, axis=-1, keepdims=True)
        idx = jnp.min(jnp.where(work == mx, lane, n_exp), axis=-1, keepdims=True)
        sel = lane == idx
        vals.append(mx)
        idxs.append(idx)
        sels.append(sel)
        work = jnp.where(sel, -jnp.inf, work)
    es = [jnp.exp(v - vals[0]) for v in vals]
    inv = 1.0 / functools.reduce(lambda a, b: a + b, es)

    hot = functools.reduce(lambda a, b: a + b, [s.astype(F32) for s in sels])
    ri = lax.broadcasted_iota(jnp.int32, (tm, tm), 0)
    rj = lax.broadcasted_iota(jnp.int32, (tm, tm), 1)
    earlier = (rj < ri).astype(BF16)
    base = jnp.dot(earlier, hot.astype(BF16), preferred_element_type=F32) + carry_ref[...]
    ranks = [jnp.sum(jnp.where(s, base, 0.0), axis=-1, keepdims=True) for s in sels]
    carry_ref[...] = carry_ref[...] + jnp.sum(hot, axis=0, keepdims=True)
    cnt_ref[...] = carry_ref[...]

    ol = lax.broadcasted_iota(jnp.int32, (tm, ROUTE_LANES), 1)
    out = jnp.zeros((tm, ROUTE_LANES), F32)
    for k in range(TOP_K):
        out = jnp.where(ol == k, idxs[k].astype(F32), out)
        out = jnp.where(ol == TOP_K + k, ranks[k], out)
        out = jnp.where(ol == 2 * TOP_K + k, es[k] * inv, out)
    route_ref[...] = out


def _router(h, g, w_r, b_r, counts0, *, tm, name):
    t, d = h.shape
    n_exp = w_r.shape[1]
    return pl.pallas_call(
        _router_kernel,
        grid=(t // tm,),
        in_specs=[
            pl.BlockSpec((tm, d), lambda i: (i, 0)),
            pl.BlockSpec((1, d), lambda i: (0, 0)),
            pl.BlockSpec((d, n_exp), lambda i: (0, 0)),
            pl.BlockSpec((1, n_exp), lambda i: (0, 0)),
            pl.BlockSpec((1, n_exp), lambda i: (0, 0)),
        ],
        out_specs=[
            pl.BlockSpec((tm, d), lambda i: (i, 0)),
            pl.BlockSpec((tm, ROUTE_LANES), lambda i: (i, 0)),
            pl.BlockSpec((1, n_exp), lambda i: (0, 0)),
        ],
        out_shape=[
            jax.ShapeDtypeStruct((t, d), F32),
            jax.ShapeDtypeStruct((t, ROUTE_LANES), F32),
            jax.ShapeDtypeStruct((1, n_exp), F32),
        ],
        scratch_shapes=[pltpu.VMEM((1, n_exp), F32)],
        compiler_params=_params("arbitrary"),
        name=name,
    )(h, g.reshape(1, d), w_r, b_r.reshape(1, n_exp), counts0)


def _scatter_kernel(*refs, zero_fill, n_exp):
    if zero_fill:
        pstart_ref, pend_ref, dest_ref, m_ref, rows_hbm, zeros_ref, sem = refs
    else:
        pstart_ref, pend_ref, dest_ref, m_ref, _, rows_hbm, zeros_ref, sem = refs
    ts = dest_ref.shape[0] // TOP_K
    i = pl.program_id(0)

    if zero_fill:
        @pl.when(i == 0)
        def _():
            zeros_ref[...] = jnp.zeros_like(zeros_ref)

            def fill(e):
                return pltpu.make_async_copy(
                    zeros_ref, rows_hbm.at[pl.ds(pl.multiple_of(pend_ref[e] - MOE_CHUNK, MOE_CHUNK), MOE_CHUNK)], sem)

            def start(e, carry):
                @pl.when(pend_ref[e] > pstart_ref[e])
                def _():
                    fill(e).start()
                return carry

            def wait(e, carry):
                @pl.when(pend_ref[e] > pstart_ref[e])
                def _():
                    fill(e).wait()
                return carry

            lax.fori_loop(0, n_exp, start, 0)
            lax.fori_loop(0, n_exp, wait, 0)

    def row_copy(src, dst):
        return pltpu.make_async_copy(m_ref.at[pl.ds(src, 1)], rows_hbm.at[pl.ds(dst, 1)], sem)

    def start(t, carry):
        for k in range(TOP_K):
            row_copy(t, dest_ref[TOP_K * t + k]).start(priority=k % 2)
        return carry

    def wait(t, carry):
        for k in range(TOP_K):
            row_copy(0, 0).wait()
        return carry

    unroll = min(ts, 8)
    lax.fori_loop(0, ts, start, 0, unroll=unroll)
    lax.fori_loop(0, ts, wait, 0, unroll=unroll)


def _scatter_rows(m, dest, pstart, pend, rows_in, *, ts, n_rows, name):
    t, d = m.shape
    zero_fill = rows_in is None
    n_exp = pstart.shape[0]
    in_specs = [
        pl.BlockSpec((ts * TOP_K,), lambda i, *_: (i,), memory_space=pltpu.SMEM),
        pl.BlockSpec((ts, d), lambda i, *_: (i, 0)),
    ]
    args = [dest, m]
    aliases = {}
    if not zero_fill:
        in_specs.append(pl.BlockSpec(memory_space=pl.ANY))
        args.append(rows_in)
        aliases = {4: 0}
    return pl.pallas_call(
        functools.partial(_scatter_kernel, zero_fill=zero_fill, n_exp=n_exp),
        grid_spec=pltpu.PrefetchScalarGridSpec(
            num_scalar_prefetch=2,
            grid=(t // ts,),
            in_specs=in_specs,
            out_specs=pl.BlockSpec(memory_space=pl.ANY),
            scratch_shapes=[pltpu.VMEM((MOE_CHUNK, d), F32), pltpu.SemaphoreType.DMA(())],
        ),
        out_shape=jax.ShapeDtypeStruct((n_rows, d), F32),
        input_output_aliases=aliases,
        compiler_params=_params("arbitrary"),
        name=name,
    )(pstart, pend, *args)


def _expert_kernel(ir_ref, inch_ref, ie_ref, nv_ref,
                   rows_hbm, wgu_hbm, wd_hbm, bg_ref, bl_ref, bd_ref, ys_hbm,
                   stage, xb, hdn, ybuf, wring, sem_in, sem_out, sem_w, *, n_ff, n_dt):
    i = pl.program_id(0)
    s = pl.program_id(1)
    n_items = pl.num_programs(0)
    n_steps = n_ff + n_dt
    nch = inch_ref[i]
    tile = ybuf.shape[2]
    d_ff = n_ff * tile
    step_id = i * n_steps + s
    live_steps = nv_ref[0] * n_steps
    slot = lax.rem(step_id, WEIGHT_RING)

    def weight_copies(item, step, ring_slot, first_phase):
        e = ie_ref[item]

        def tile_copy(src, col, k):
            return pltpu.make_async_copy(src.at[e, :, pl.ds(pl.multiple_of(col, tile), tile)],
                                         wring.at[ring_slot, k], sem_w.at[ring_slot, k])

        if first_phase:
            return [tile_copy(wgu_hbm, step * tile, 0), tile_copy(wgu_hbm, d_ff + step * tile, 1)]
        return [tile_copy(wd_hbm, (step - n_ff) * tile, 0)]

    def weights(op, item, step, ring_slot):
        @pl.when(step < n_ff)
        def _():
            for cp in weight_copies(item, step, ring_slot, True):
                op(cp)

        @pl.when(step >= n_ff)
        def _():
            for cp in weight_copies(item, step, ring_slot, False):
                op(cp)

    def fetch(ahead):
        tgt = step_id + ahead

        @pl.when(tgt < live_steps)
        def _():
            weights(lambda cp: cp.start(), lax.div(tgt, n_steps), lax.rem(tgt, n_steps),
                    lax.rem(tgt, WEIGHT_RING))

    @pl.when(step_id == 0)
    def _():
        for ahead in range(WEIGHT_RING - 1):
            fetch(ahead)

    fetch(WEIGHT_RING - 1)

    def chunk(c):
        return slice(c * MOE_CHUNK, (c + 1) * MOE_CHUNK)

    def row_window(item, c):
        return pl.ds(pl.multiple_of(ir_ref[item], MOE_CHUNK) + c * MOE_CHUNK, MOE_CHUNK)

    def load(item, c):
        return pltpu.make_async_copy(rows_hbm.at[row_window(item, c)], stage.at[chunk(c)], sem_in.at[c])

    def store(item, c, n):
        return pltpu.make_async_copy(
            ybuf.at[n, chunk(c)], ys_hbm.at[row_window(item, c), pl.ds(n * tile, tile)], sem_out.at[c])

    def for_chunks(count, fn):
        for c in range(MOE_MAX_CHUNKS):
            pl.when(c < count)(functools.partial(fn, c))

    def for_sizes(fn):
        for nc in range(1, MOE_MAX_CHUNKS + 1):
            pl.when(nch == nc)(functools.partial(fn, nc * MOE_CHUNK))

    @pl.when(s == 0)
    def _():
        @pl.when(i == 0)
        def _():
            for_chunks(nch, lambda c: load(0, c).start())

        def land(c):
            load(i, c).wait()
            xb[chunk(c), :] = stage[chunk(c), :].astype(BF16)

        for_chunks(nch, land)

    @pl.when(nch > 0)
    def _():
        weights(lambda cp: cp.wait(), i, s, slot)

    @pl.when((s < n_ff) & (nch > 0))
    def _():
        def phase1(rows):
            wg = wring[slot, 0].astype(BF16)
            wl = wring[slot, 1].astype(BF16)
            x = xb[:rows, :]
            gate = jnp.dot(x, wg, preferred_element_type=F32) + bg_ref[...]
            lin = jnp.dot(x, wl, preferred_element_type=F32) + bl_ref[...]
            gate = jnp.minimum(gate, SWIGLU_LIMIT)
            lin = jnp.clip(lin, -SWIGLU_LIMIT, SWIGLU_LIMIT)
            act = gate * jax.nn.sigmoid(SWIGLU_ALPHA * gate) * (lin + 1.0)
            hdn[s, :rows, :] = act.astype(BF16)

        for_sizes(phase1)

    @pl.when(s == n_ff)
    def _():
        @pl.when(i > 0)
        def _():
            def drain(c):
                for n in range(n_dt):
                    store(i - 1, c, n).wait()

            for_chunks(inch_ref[jnp.maximum(i - 1, 0)], drain)

        @pl.when(i + 1 < n_items)
        def _():
            nxt = jnp.minimum(i + 1, n_items - 1)
            for_chunks(inch_ref[nxt], lambda c: load(nxt, c).start())

    @pl.when((s >= n_ff) & (nch > 0))
    def _():
        n = s - n_ff

        def phase2(rows):
            wd = wring[slot, 0].astype(BF16)
            y = bd_ref[...]
            for f in range(n_ff):
                y = y + jnp.dot(hdn[f, :rows, :], wd[f * tile:(f + 1) * tile, :],
                                preferred_element_type=F32)
            ybuf[n, :rows, :] = y

        for_sizes(phase2)

    @pl.when(s == n_ff + n_dt - 1)
    def _():
        def flush(c):
            for n in range(n_dt):
                store(i, c, n).start()

        for_chunks(nch, flush)

        @pl.when(i == n_items - 1)
        def _():
            def drain(c):
                for n in range(n_dt):
                    store(i, c, n).wait()

            for_chunks(nch, drain)


def _expert_ffn(rows, w_gu, b_gu, w_down, b_down, item_e, item_row0, item_nch):
    n_rows, d = rows.shape
    n_exp, _, two_ff = w_gu.shape
    d_ff = two_ff // 2
    tile = MOE_FF_TILE
    n_ff = d_ff // tile
    n_dt = d // tile
    n_steps = n_ff + n_dt
    n_items = item_e.shape[0]
    r_max = MOE_MAX_CHUNKS * MOE_CHUNK

    assert d == d_ff, "one ring slot shape serves both phases' weight tiles"
    n_live = jnp.sum((item_nch > 0).astype(jnp.int32)).reshape(1)

    def gate_map(i, s, ir, inch, ie, nv):
        return ie[i], 0, jnp.minimum(s, n_ff - 1)

    def lin_map(i, s, ir, inch, ie, nv):
        return ie[i], 0, n_ff + jnp.minimum(s, n_ff - 1)

    def down_map(i, s, ir, inch, ie, nv):
        return ie[i], 0, jnp.maximum(s - n_ff, 0)

    return pl.pallas_call(
        functools.partial(_expert_kernel, n_ff=n_ff, n_dt=n_dt),
        grid_spec=pltpu.PrefetchScalarGridSpec(
            num_scalar_prefetch=4,
            grid=(n_items, n_steps),
            in_specs=[
                pl.BlockSpec(memory_space=pl.ANY),
                pl.BlockSpec(memory_space=pl.ANY),
                pl.BlockSpec(memory_space=pl.ANY),
                pl.BlockSpec((None, 1, tile), gate_map),
                pl.BlockSpec((None, 1, tile), lin_map),
                pl.BlockSpec((None, 1, tile), down_map),
            ],
            out_specs=pl.BlockSpec(memory_space=pl.ANY),
            scratch_shapes=[
                pltpu.VMEM((r_max, d), F32),
                pltpu.VMEM((r_max, d), BF16),
                pltpu.VMEM((n_ff, r_max, tile), BF16),
                pltpu.VMEM((n_dt, r_max, tile), F32),
                pltpu.VMEM((WEIGHT_RING, 2, d, tile), F32),
                pltpu.SemaphoreType.DMA((MOE_MAX_CHUNKS,)),
                pltpu.SemaphoreType.DMA((MOE_MAX_CHUNKS,)),
                pltpu.SemaphoreType.DMA((WEIGHT_RING, 2)),
            ],
        ),
        out_shape=jax.ShapeDtypeStruct((n_rows, d), F32),
        compiler_params=_params("arbitrary", "arbitrary"),
        name="expert_ffn",
    )(item_row0, item_nch, item_e.astype(jnp.int32), n_live, rows, w_gu, w_down,
      b_gu.reshape(n_exp, 1, two_ff), b_gu.reshape(n_exp, 1, two_ff),
      b_down.reshape(n_exp, 1, d))


def _combine_kernel(dest_ref, route_ref, h_ref, ys_hbm, o_ref, buf, sem):
    tg = h_ref.shape[0]

    def row_copy(src, k, t):
        return pltpu.make_async_copy(ys_hbm.at[pl.ds(src, 1)], buf.at[k, pl.ds(t, 1)], sem)

    def start(t, carry):
        for k in range(TOP_K):
            row_copy(dest_ref[TOP_K * t + k], k, t).start(priority=k % 2)
        return carry

    def wait(t, carry):
        for k in range(TOP_K):
            row_copy(0, k, 0).wait()
        return carry

    unroll = min(tg, 8)
    lax.fori_loop(0, tg, start, 0, unroll=unroll)
    lax.fori_loop(0, tg, wait, 0, unroll=unroll)
    out = h_ref[...]
    for k in range(TOP_K):
        out = out + route_ref[:, 2 * TOP_K + k:2 * TOP_K + k + 1] * buf[k]
    o_ref[...] = out


def _combine(dest, route, h, ys, *, tg, name):
    t, d = h.shape
    return pl.pallas_call(
        _combine_kernel,
        grid=(t // tg,),
        in_specs=[
            pl.BlockSpec((tg * TOP_K,), lambda i: (i,), memory_space=pltpu.SMEM),
            pl.BlockSpec((tg, ROUTE_LANES), lambda i: (i, 0)),
            pl.BlockSpec((tg, d), lambda i: (i, 0)),
            pl.BlockSpec(memory_space=pl.ANY),
        ],
        out_specs=pl.BlockSpec((tg, d), lambda i: (i, 0)),
        out_shape=jax.ShapeDtypeStruct((t, d), F32),
        scratch_shapes=[pltpu.VMEM((TOP_K, tg, d), F32), pltpu.SemaphoreType.DMA(())],
        compiler_params=_params("arbitrary"),
        name=name,
    )(dest, route, h, ys)


def _ple_kernel(h_ref, p_ref, g_ref, wg_ref, wp_ref, o_ref, hn_ref):
    rows = min(h_ref.shape[0], 128)

    def body(c, carry):
        sl = pl.ds(pl.multiple_of(c * rows, rows), rows)
        h = h_ref[sl, :]
        hn_ref[sl, :] = (h * _rms_scale(h) * g_ref[...]).astype(BF16)
        return carry

    lax.fori_loop(0, h_ref.shape[0] // rows, body, 0)
    gate = jax.nn.sigmoid(jnp.dot(hn_ref[...], wg_ref[...].astype(BF16), preferred_element_type=F32))
    proj = jnp.dot(p_ref[...].astype(BF16), wp_ref[...].astype(BF16), preferred_element_type=F32)
    o_ref[...] = h_ref[...] + gate * proj


def _ple(h, p, g, w_gate, w_proj, *, tm, name):
    t, d = h.shape
    pd = p.shape[1]
    return pl.pallas_call(
        _ple_kernel,
        grid=(t // tm,),
        in_specs=[
            pl.BlockSpec((tm, d), lambda i: (i, 0)),
            pl.BlockSpec((tm, pd), lambda i: (i, 0)),
            pl.BlockSpec((1, d), lambda i: (0, 0)),
            pl.BlockSpec((d, d), lambda i: (0, 0)),
            pl.BlockSpec((pd, d), lambda i: (0, 0)),
        ],
        out_specs=pl.BlockSpec((tm, d), lambda i: (i, 0)),
        out_shape=jax.ShapeDtypeStruct((t, d), F32),
        scratch_shapes=[pltpu.VMEM((tm, d), BF16)],
        compiler_params=_params("parallel"),
        name=name,
    )(h, p, g.reshape(1, d), w_gate, w_proj)


def _decode_kernel(*refs, groups, hpg):
    ng = len(groups)
    z_ref = refs[0]
    caches = refs[1:1 + ng]
    (cos_ref, sa_ref, sb_ref, gq_ref, gk_ref, lng_ref, lnb_ref, ws0_ref, bs0_ref,
     attn_ref, kn_ref, gm_ref, vn_ref) = refs[1 + ng:]
    heads = ng * hpg
    aw = heads * HEAD_DIM
    scale = HEAD_DIM ** -0.5
    tabs = (cos_ref[...], sa_ref[...], sb_ref[...])

    def head(base, hh):
        return z_ref[:, base + hh * HEAD_DIM:base + (hh + 1) * HEAD_DIM]

    outs, lses = [], []
    for g in range(ng):
        c_ref = caches[g]
        for j in range(hpg):
            hh = g * hpg + j
            qn = _norm_rope(head(0, hh), gq_ref[...], *tabs) * scale
            kn = _norm_rope(head(aw, hh), gk_ref[...], *tabs)
            vv = head(2 * aw, hh)
            kn_ref[:, hh * HEAD_DIM:(hh + 1) * HEAD_DIM] = kn
            kc = c_ref[:, 0, j, :]
            vc = c_ref[:, 1, j, :]
            s_c = jnp.sum(kc * qn, axis=-1, keepdims=True)
            s_n = jnp.sum(kn * qn, axis=-1, keepdims=True)
            m = jnp.maximum(jnp.max(s_c, axis=0, keepdims=True), s_n)
            p_c = jnp.exp(s_c - m)
            p_n = jnp.exp(s_n - m)
            den = jnp.sum(p_c, axis=0, keepdims=True) + p_n
            o = (jnp.sum(p_c * vc, axis=0, keepdims=True) + p_n * vv) / den
            outs.append(o)
            lses.append(m + jnp.log(den))

    for j in range(hpg):
        ls = [lses[g * hpg + j] for g in range(ng)]
        mx = functools.reduce(jnp.maximum, ls)
        es = [jnp.exp(l - mx) for l in ls]
        inv = 1.0 / functools.reduce(lambda a, b: a + b, es)
        for g in range(ng):
            hh = g * hpg + j
            attn_ref[:, hh * HEAD_DIM:(hh + 1) * HEAD_DIM] = outs[hh] * (es[g] * inv)

    gw = gm_ref.shape[1]
    for g in range(gw // GMLP_CH):
        cols = slice(g * GMLP_CH, (g + 1) * GMLP_CH)
        v = _gelu(z_ref[:, 3 * aw + gw + g * GMLP_CH:3 * aw + gw + (g + 1) * GMLP_CH])
        mu = jnp.mean(v, axis=-1, keepdims=True)
        vc = v - mu
        var = jnp.mean(vc * vc, axis=-1, keepdims=True)
        vn = vc * lax.rsqrt(var + EPS) * lng_ref[:, cols] + lnb_ref[:, cols]
        vn_ref[:, cols] = vn
        u = _gelu(z_ref[:, 3 * aw + g * GMLP_CH:3 * aw + (g + 1) * GMLP_CH])
        gm_ref[:, cols] = u * (ws0_ref[:, cols] * vn + bs0_ref[:, cols])


def _sample_mixers(zs, caches, tables, g_q, g_k, ln_g, ln_b, w_s, b_s, *, hpg):
    bsz, in_width = zs.shape
    ng = len(DILATION_GROUPS)
    aw = ng * hpg * HEAD_DIM
    gw = (in_width - 3 * aw) // 2
    in_specs = [pl.BlockSpec((None, 1, in_width), lambda b: (b, 0, 0))]
    cache_args = []
    for (window, dil), c in zip(DILATION_GROUPS, caches):
        wc = c.shape[-4]
        assert wc == window and c.shape[-5] == bsz, "window caches must hold a full window"
        steps = wc // dil
        cache_args.append(c.reshape(bsz, steps, dil, 2, hpg, HEAD_DIM))
        in_specs.append(pl.BlockSpec((None, steps, None, 2, hpg, HEAD_DIM),
                                     lambda b: (b, 0, 0, 0, 0, 0)))
    vec = pl.BlockSpec((1, HEAD_DIM), lambda b: (0, 0))
    wide = pl.BlockSpec((1, gw), lambda b: (0, 0))
    in_specs += [vec] * 5 + [wide] * 4

    def out(w):
        return pl.BlockSpec((None, 1, w), lambda b: (b, 0, 0)), jax.ShapeDtypeStruct((bsz, 1, w), F32)

    specs, shapes = zip(out(aw), out(aw), out(gw), out(gw))
    groups = w_s.shape[0]
    ws0 = jnp.repeat(w_s[:, 0, 0], GMLP_CH).reshape(1, gw)
    bs0 = jnp.repeat(b_s[:, 0], GMLP_CH).reshape(1, gw)
    assert groups * GMLP_CH == gw
    res = pl.pallas_call(
        functools.partial(_decode_kernel, groups=DILATION_GROUPS, hpg=hpg),
        grid=(bsz,),
        in_specs=in_specs,
        out_specs=list(specs),
        out_shape=list(shapes),
        compiler_params=_params("parallel"),
        name="sample_mixers",
    )(zs.reshape(bsz, 1, in_width), *cache_args, *tables,
      g_q.reshape(1, HEAD_DIM), g_k.reshape(1, HEAD_DIM),
      ln_g.reshape(1, gw), ln_b.reshape(1, gw), ws0, bs0)
    return [r.reshape(bsz, -1) for r in res]


def _expert_layout(counts, n_items):
    counts = counts.astype(jnp.int32)
    padded = (counts + MOE_CHUNK - 1) // MOE_CHUNK * MOE_CHUNK
    pend = jnp.cumsum(padded)
    pstart = pend - padded
    r_max = MOE_MAX_CHUNKS * MOE_CHUNK
    per_e = (padded + r_max - 1) // r_max
    item_end = jnp.cumsum(per_e)
    total = item_end[-1]
    ids = jnp.arange(n_items, dtype=jnp.int32)
    clamped = jnp.minimum(ids, total - 1)
    e = jnp.sum((item_end[None, :] <= clamped[:, None]).astype(jnp.int32), axis=1)
    local = clamped - (item_end[e] - per_e[e])
    row0 = pstart[e] + local * r_max
    nch = jnp.minimum(padded[e] - local * r_max, r_max) // MOE_CHUNK
    nch = jnp.where(ids < total, nch, 0)
    return pstart, pend, e, row0.astype(jnp.int32), nch.astype(jnp.int32)


def _destinations(route, pstart):
    e = route[:, :TOP_K].astype(jnp.int32)
    rank = route[:, TOP_K:2 * TOP_K].astype(jnp.int32)
    return (pstart[e] + rank).reshape(-1)


def kernel(x_prompt, x_sample, cache_kv_w128, cache_kv_w512, cache_kv_w2048, p_prompt, p_sample,
           g_mix, w_in, g_q, g_k, gmlp_ln_g, gmlp_ln_b, gmlp_w_s, gmlp_b_s, g_out_attn, g_out_gmlp,
           w_out, g_ffn, w_router, b_router, w_gu, b_gu, w_down, b_down, g_ple, w_ple_gate,
           w_ple_proj):
    depth = g_mix.shape[0]
    assert depth == 1, "single-layer step"
    batch, seq, d = x_prompt.shape
    dec_batch, dec_seq, _ = x_sample.shape
    assert dec_seq == 1
    in_width = w_in.shape[2]
    gw = gmlp_ln_g.shape[1] * gmlp_ln_g.shape[2]
    aw = (in_width - 2 * gw) // 3
    ng = len(DILATION_GROUPS)
    hpg = aw // HEAD_DIM // ng
    n_exp = w_router.shape[2]
    tp = batch * seq
    ts = dec_batch * dec_seq

    def layer(a):
        return a.reshape(a.shape[1:])

    caches = (cache_kv_w128, cache_kv_w512, cache_kv_w2048)
    (p_prompt, p_sample, g_mix, w_in, g_q, g_k, gmlp_ln_g, gmlp_ln_b, gmlp_w_s, gmlp_b_s,
     g_out_attn, g_out_gmlp, w_out, g_ffn, w_router, b_router, w_gu, b_gu, w_down, b_down, g_ple,
     w_ple_gate, w_ple_proj) = [layer(a) for a in (
         p_prompt, p_sample, g_mix, w_in, g_q, g_k, gmlp_ln_g, gmlp_ln_b, gmlp_w_s, gmlp_b_s,
         g_out_attn, g_out_gmlp, w_out, g_ffn, w_router, b_router, w_gu, b_gu, w_down, b_down,
         g_ple, w_ple_gate, w_ple_proj)]

    w_in_b, w_out_b = w_in.astype(BF16), w_out.astype(BF16)
    w_ple_gate_b, w_ple_proj_b = w_ple_gate.astype(BF16), w_ple_proj.astype(BF16)
    xp = x_prompt.reshape(tp, d)
    z = _norm_matmul(xp, g_mix, w_in_b, tm=512, tn=in_width // 2, name="prompt_in_proj",
                     rows_inner=True)
    z3 = z.reshape(batch, seq, in_width)
    tables_p = _rope_tables(jnp.arange(seq, dtype=jnp.int32))
    attn_p, kn_p = _prompt_attention(z3, g_q, g_k, tables_p, batch=batch, seq=seq, hpg=hpg)
    gm_p = _prompt_gmlp(z3, gmlp_ln_g, gmlp_ln_b, gmlp_w_s, gmlp_b_s,
                        batch=batch, seq=seq, width=gw, col_block=3 * aw // gw)
    hp = _out_proj([a.reshape(tp, -1) for a in attn_p], gm_p.reshape(tp, gw), xp,
                   g_out_attn, g_out_gmlp, w_out_b, tm=512, tn=d, name="prompt_out_proj")

    xs = x_sample.reshape(ts, d)
    zs = _norm_matmul(xs, g_mix, w_in, tm=ts, tn=512, name="sample_in_proj", exact=True)
    tables_s = _rope_tables(PAST_LEN + jnp.arange(dec_seq, dtype=jnp.int32))
    attn_s, kn_s, gm_s, vn_s = _sample_mixers(
        zs, caches, tables_s, g_q, g_k, gmlp_ln_g, gmlp_ln_b, gmlp_w_s, gmlp_b_s, hpg=hpg)
    gwid = hpg * HEAD_DIM
    hs = _out_proj([attn_s[:, g * gwid:(g + 1) * gwid] for g in range(ng)], gm_s, xs,
                   g_out_attn, g_out_gmlp, w_out, tm=ts, tn=512, name="sample_out_proj",
                   exact=True)

    zero_counts = jnp.zeros((1, n_exp), F32)
    m_p, route_p, counts_p = _router(hp, g_ffn, w_router, b_router, zero_counts,
                                     tm=512, name="prompt_router")
    m_s, route_s, counts = _router(hs, g_ffn, w_router, b_router, counts_p,
                                   tm=ts, name="sample_router")
    n_assign = (tp + ts) * TOP_K
    n_chunks = -(-n_assign // MOE_CHUNK) + n_exp
    n_rows = n_chunks * MOE_CHUNK
    n_items = n_exp + -(-n_chunks // MOE_MAX_CHUNKS)
    pstart, pend, item_e, item_row0, item_nch = _expert_layout(counts.reshape(n_exp), n_items)
    dest_p = _destinations(route_p, pstart)
    dest_s = _destinations(route_s, pstart)
    rows = _scatter_rows(m_p, dest_p, pstart, pend, None, ts=512, n_rows=n_rows,
                         name="prompt_scatter")
    rows = _scatter_rows(m_s, dest_s, pstart, pend, rows, ts=ts, n_rows=n_rows,
                         name="sample_scatter")
    ys = _expert_ffn(rows, w_gu, b_gu, w_down, b_down, item_e, item_row0, item_nch)
    hp = _combine(dest_p, route_p, hp, ys, tg=256, name="prompt_combine")
    hs = _combine(dest_s, route_s, hs, ys, tg=ts, name="sample_combine")

    yp = _ple(hp, p_prompt.reshape(tp, -1), g_ple, w_ple_gate_b, w_ple_proj_b,
              tm=512, name="prompt_ple")
    ysm = _ple(hs, p_sample.reshape(ts, -1), g_ple, w_ple_gate_b, w_ple_proj_b,
               tm=ts, name="sample_ple")

    v0 = 2 * aw
    kv_prompt, kv_sample = [], []
    for g, (window, _) in enumerate(DILATION_GROUPS):
        wc = min(window, seq)
        kg = kn_p[g][:, seq - wc:].reshape(batch, wc, hpg, HEAD_DIM)
        vg = z3[:, seq - wc:, v0 + g * gwid:v0 + (g + 1) * gwid].reshape(batch, wc, hpg, HEAD_DIM)
        kv_prompt.append(jnp.stack([kg, vg], axis=2)[None])
        ksg = kn_s[:, g * gwid:(g + 1) * gwid].reshape(dec_batch, dec_seq, hpg, HEAD_DIM)
        vsg = zs[:, v0 + g * gwid:v0 + (g + 1) * gwid].reshape(dec_batch, dec_seq, hpg, HEAD_DIM)
        kv_sample.append(jnp.stack([ksg, vsg], axis=2)[None])
    gmlp_v = vn_s.reshape(1, dec_batch, dec_seq, gw // GMLP_CH, GMLP_CH)
    return (yp.reshape(batch, seq, d), ysm.reshape(dec_batch, dec_seq, d),
            *kv_prompt, *kv_sample, gmlp_v)
```

```python
import functools

import jax
import jax.numpy as jnp
from jax import lax
from jax.experimental import pallas as pl
from jax.experimental.pallas import tpu as pltpu

F32 = jnp.float32
BF16 = jnp.bfloat16

EPS = 1e-6
HEAD_DIM = 128
ROT_DIM = HEAD_DIM // 4
ROPE_THETA = 500000.0
DILATION_GROUPS = ((128, 1), (512, 4), (2048, 16))
PAST_LEN = 16384
QBLK = 128
CHUNK = 128
GMLP_CH = 128
TOP_K = 4
SWIGLU_LIMIT = 7.0
SWIGLU_ALPHA = 1.702

V7X_VMEM_BYTES = 64 * 1024 * 1024
VMEM_LIMIT = V7X_VMEM_BYTES - 8 * 1024 * 1024

MOE_CHUNK = 256
MOE_MAX_CHUNKS = 5
MOE_FF_TILE = 256
WEIGHT_RING = 3


def _params(*sem):
    return pltpu.CompilerParams(dimension_semantics=sem, vmem_limit_bytes=VMEM_LIMIT)


def _split_bf16(a):
    hi = a.astype(BF16)
    lo = (a - hi.astype(F32)).astype(BF16)
    return hi, lo


def _mm(a, b, exact=False):
    d = functools.partial(jnp.dot, preferred_element_type=F32)
    if not exact:
        return d(a.astype(BF16), b.astype(BF16))
    ah, al = _split_bf16(a)
    bh, bl = _split_bf16(b)
    return d(ah, bh) + (d(ah, bl) + d(al, bh))


def _gelu(x):
    return 0.5 * x * (1.0 + jnp.tanh(0.7978845608028654 * (x + 0.044715 * (x * x * x))))


def _rms_scale(x):
    return lax.rsqrt(jnp.mean(x * x, axis=-1, keepdims=True) + EPS)


def _norm_matmul_kernel(x_ref, g_ref, w_ref, o_ref, xn_ref, *, exact, rows_inner):
    def normalise():
        rows = min(x_ref.shape[0], 128)

        def body(c, carry):
            sl = pl.ds(pl.multiple_of(c * rows, rows), rows)
            x = x_ref[sl, :]
            xn_ref[sl, :] = (x * _rms_scale(x) * g_ref[...]).astype(xn_ref.dtype)
            return carry

        lax.fori_loop(0, x_ref.shape[0] // rows, body, 0)

    if rows_inner:
        normalise()
    else:
        pl.when(pl.program_id(1) == 0)(normalise)

    o_ref[...] = _mm(xn_ref[...], w_ref[...], exact)


def _norm_matmul(x, g, w, *, tm, tn, name, exact=False, rows_inner=False):
    m, k = x.shape
    n = w.shape[1]
    if rows_inner:
        grid = (n // tn, m // tm)
        row, col = (lambda j, i: i), (lambda j, i: j)
    else:
        grid = (m // tm, n // tn)
        row, col = (lambda i, j: i), (lambda i, j: j)
    return pl.pallas_call(
        functools.partial(_norm_matmul_kernel, exact=exact, rows_inner=rows_inner),
        grid=grid,
        in_specs=[
            pl.BlockSpec((tm, k), lambda a, b: (row(a, b), 0)),
            pl.BlockSpec((1, k), lambda a, b: (0, 0)),
            pl.BlockSpec((k, tn), lambda a, b: (0, col(a, b))),
        ],
        out_specs=pl.BlockSpec((tm, tn), lambda a, b: (row(a, b), col(a, b))),
        out_shape=jax.ShapeDtypeStruct((m, n), F32),
        scratch_shapes=[pltpu.VMEM((tm, k), F32 if exact else BF16)],
        compiler_params=_params("parallel", "arbitrary"),
        name=name,
    )(x, g.reshape(1, k), w)


def _rope_tables(pos):
    half = ROT_DIM // 2
    inv = 1.0 / (ROPE_THETA ** (jnp.arange(half, dtype=F32) * (2.0 / ROT_DIM)))
    ang = pos.astype(F32)[:, None] * inv[None, :]
    cos, sin = jnp.cos(ang), jnp.sin(ang)
    t = pos.shape[0]
    pad = HEAD_DIM - ROT_DIM
    cos_t = jnp.concatenate([cos, cos, jnp.ones((t, pad), F32)], axis=1)
    sa = jnp.concatenate([-sin, jnp.zeros((t, HEAD_DIM - half), F32)], axis=1)
    sb = jnp.concatenate([jnp.zeros((t, half), F32), sin, jnp.zeros((t, pad), F32)], axis=1)
    return cos_t, sa, sb


def _norm_rope(x, gain, cos_t, sa, sb):
    half = ROT_DIM // 2
    xn = x * _rms_scale(x) * gain
    up = pltpu.roll(xn, HEAD_DIM - half, 1)
    dn = pltpu.roll(xn, half, 1)
    return xn * cos_t + up * sa + dn * sb


def _band_block(q, k, v, mask):
    s = lax.dot_general(q.astype(BF16), k.astype(BF16), (((1,), (1,)), ((), ())),
                        preferred_element_type=F32)
    s = jnp.where(mask, s, -jnp.inf)
    m = jnp.max(s, axis=-1, keepdims=True)
    p = jnp.exp(s - m)
    den = jnp.sum(p, axis=-1, keepdims=True)
    o = jnp.dot((p / den).astype(BF16), v.astype(BF16), preferred_element_type=F32)
    return o, m + jnp.log(den)


def _attn_kernel(*refs, seq, groups):
    ng = len(groups)
    qkv = refs[:3 * ng]
    cos_ref, sa_ref, sb_ref, gq_ref, gk_ref = refs[3 * ng:3 * ng + 5]
    outs = refs[3 * ng + 5:3 * ng + 5 + 2 * ng]
    o_refs, kn_refs = outs[:ng], outs[ng:]
    qs, ks = refs[3 * ng + 5 + 2 * ng:3 * ng + 7 + 2 * ng]
    ls_refs = refs[3 * ng + 7 + 2 * ng:]
    scale = HEAD_DIM ** -0.5
    nrm_rows = 256

    qi = lax.broadcasted_iota(jnp.int32, (QBLK, 2 * QBLK), 0)
    kj = lax.broadcasted_iota(jnp.int32, (QBLK, 2 * QBLK), 1)
    qi1 = lax.broadcasted_iota(jnp.int32, (QBLK, QBLK), 0)
    kj1 = lax.broadcasted_iota(jnp.int32, (QBLK, QBLK), 1)

    for g, (window, dil) in enumerate(groups):
        q_ref, k_ref, v_ref = qkv[3 * g:3 * g + 3]
        o_ref, kn_ref, ls_ref = o_refs[g], kn_refs[g], ls_refs[g]
        n_steps = window // dil
        nb = seq // dil // QBLK

        def prep(c, carry, q_ref=q_ref, k_ref=k_ref, kn_ref=kn_ref):
            sl = pl.ds(pl.multiple_of(c * nrm_rows, nrm_rows), nrm_rows)
            tabs = (cos_ref[sl, :], sa_ref[sl, :], sb_ref[sl, :])
            qs[sl, :] = _norm_rope(q_ref[sl, :], gq_ref[...], *tabs) * scale
            kn = _norm_rope(k_ref[sl, :], gk_ref[...], *tabs)
            kn_ref[sl, :] = kn
            ks[sl, :] = kn
            return carry

        lax.fori_loop(0, seq // nrm_rows, prep, 0, unroll=2)

        def rows(start, size, dil=dil):
            return pl.ds(start, size) if dil == 1 else pl.ds(start, size, stride=dil)

        band_mask = (kj >= qi + (QBLK - n_steps)) & (kj <= qi + QBLK)
        first_mask = (kj1 <= qi1) & (qi1 - kj1 <= n_steps)

        def one_class(r, carry, v_ref=v_ref, o_ref=o_ref, ls_ref=ls_ref, rows=rows,
                      band_mask=band_mask, first_mask=first_mask, dil=dil, nb=nb):
            sl0 = rows(r, QBLK)
            o, lse = _band_block(qs[sl0, :], ks[sl0, :], v_ref[sl0, :], first_mask)
            o_ref[sl0, :] = o
            ls_ref[sl0, :] = jnp.broadcast_to(lse, (QBLK, HEAD_DIM))

            def one_block(lb, carry2):
                q0 = r + dil * QBLK * lb
                k0 = q0 - dil * QBLK
                slq, slk = rows(q0, QBLK), rows(k0, 2 * QBLK)
                o, lse = _band_block(qs[slq, :], ks[slk, :], v_ref[slk, :], band_mask)
                o_ref[slq, :] = o
                ls_ref[slq, :] = jnp.broadcast_to(lse, (QBLK, HEAD_DIM))
                return carry2

            lax.fori_loop(1, nb, one_block, 0, unroll=min(5, max(nb - 1, 1)))
            return carry

        lax.fori_loop(0, dil, one_class, 0, unroll=min(8, dil) if nb == 1 else 1)

    def merge(c, carry):
        sl = pl.ds(pl.multiple_of(c * nrm_rows, nrm_rows), nrm_rows)
        ls = [ls_refs[g][sl, :] for g in range(ng)]
        mx = functools.reduce(jnp.maximum, ls)
        es = [jnp.exp(l - mx) for l in ls]
        inv = 1.0 / functools.reduce(lambda a, b: a + b, es)
        for g in range(ng):
            o_refs[g][sl, :] = o_refs[g][sl, :] * (es[g] * inv)
        return carry

    lax.fori_loop(0, seq // nrm_rows, merge, 0)


def _prompt_attention(z, g_q, g_k, tables, *, batch, seq, hpg):
    ng = len(DILATION_GROUPS)
    heads = ng * hpg
    blk = (None, seq, HEAD_DIM)

    def col(c):
        return pl.BlockSpec(blk, lambda b, h, c=c: (b, 0, c + h))

    in_specs = []
    for g in range(ng):
        in_specs += [col(g * hpg), col(heads + g * hpg), col(2 * heads + g * hpg)]
    tab = pl.BlockSpec((seq, HEAD_DIM), lambda b, h: (0, 0))
    vec = pl.BlockSpec((1, HEAD_DIM), lambda b, h: (0, 0))
    in_specs += [tab, tab, tab, vec, vec]
    out_spec = pl.BlockSpec(blk, lambda b, h: (b, 0, h))
    out_sds = jax.ShapeDtypeStruct((batch, seq, hpg * HEAD_DIM), F32)
    res = pl.pallas_call(
        functools.partial(_attn_kernel, seq=seq, groups=DILATION_GROUPS),
        grid=(batch, hpg),
        in_specs=in_specs,
        out_specs=[out_spec] * (2 * ng),
        out_shape=[out_sds] * (2 * ng),
        scratch_shapes=[pltpu.VMEM((seq, HEAD_DIM), F32)] * (2 + ng),
        compiler_params=_params("parallel", "arbitrary"),
        name="prompt_attention",
    )(*([z] * (3 * ng)), *tables, g_q.reshape(1, HEAD_DIM), g_k.reshape(1, HEAD_DIM))
    return res[:ng], res[ng:]


def _gmlp_kernel(u_ref, vg_ref, lng_ref, lnb_ref, ws_ref, bs_ref, o_ref):
    groups = ws_ref.shape[0]
    ci = lax.broadcasted_iota(jnp.int32, (CHUNK, CHUNK), 0)
    cj = lax.broadcasted_iota(jnp.int32, (CHUNK, CHUNK), 1)
    for g in range(groups):
        cols = slice(g * GMLP_CH, (g + 1) * GMLP_CH)
        ws = jnp.where(cj <= ci, ws_ref[g], 0.0).astype(BF16)
        bias = bs_ref[:, g:g + 1]
        for n in range(u_ref.shape[0] // CHUNK):
            rows = slice(n * CHUNK, (n + 1) * CHUNK)
            v = _gelu(vg_ref[rows, cols])
            mu = jnp.mean(v, axis=-1, keepdims=True)
            vc = v - mu
            var = jnp.mean(vc * vc, axis=-1, keepdims=True)
            vn = vc * lax.rsqrt(var + EPS) * lng_ref[:, cols] + lnb_ref[:, cols]
            mixed = jnp.dot(ws, vn.astype(BF16), preferred_element_type=F32) + bias
            o_ref[rows, cols] = _gelu(u_ref[rows, cols]) * mixed


def _prompt_gmlp(z, ln_g, ln_b, w_s, b_s, *, batch, seq, width, col_block):
    tc = 4 * CHUNK
    groups = w_s.shape[0]
    return pl.pallas_call(
        _gmlp_kernel,
        grid=(batch, seq // tc),
        in_specs=[
            pl.BlockSpec((None, tc, width), lambda b, n: (b, n, col_block)),
            pl.BlockSpec((None, tc, width), lambda b, n: (b, n, col_block + 1)),
            pl.BlockSpec((1, width), lambda b, n: (0, 0)),
            pl.BlockSpec((1, width), lambda b, n: (0, 0)),
            pl.BlockSpec((groups, CHUNK, CHUNK), lambda b, n: (0, 0, 0)),
            pl.BlockSpec((CHUNK, groups), lambda b, n: (0, 0)),
        ],
        out_specs=pl.BlockSpec((None, tc, width), lambda b, n: (b, n, 0)),
        out_shape=jax.ShapeDtypeStruct((batch, seq, width), F32),
        compiler_params=_params("parallel", "parallel"),
        name="prompt_gmlp",
    )(z, z, ln_g.reshape(1, width), ln_b.reshape(1, width), w_s, b_s.T)


def _out_proj_kernel(*refs, n_attn, exact):
    a_refs = refs[:n_attn]
    gm_ref, x_ref, ga_ref, gg_ref, w_ref, o_ref, yn_ref = refs[n_attn:]
    wa = a_refs[0].shape[1]

    @pl.when(pl.program_id(1) == 0)
    def _():
        rows = min(x_ref.shape[0], 128)

        def body(c, carry):
            sl = pl.ds(pl.multiple_of(c * rows, rows), rows)
            parts = [a[sl, :] for a in a_refs]
            ssq = functools.reduce(lambda p, q: p + q,
                                   [jnp.sum(p * p, axis=-1, keepdims=True) for p in parts])
            rs = lax.rsqrt(ssq / (n_attn * wa) + EPS)
            for idx, p in enumerate(parts):
                cs = slice(idx * wa, (idx + 1) * wa)
                yn_ref[sl, cs] = (p * rs * ga_ref[:, cs]).astype(yn_ref.dtype)
            gm = gm_ref[sl, :]
            yn_ref[sl, n_attn * wa:] = (gm * _rms_scale(gm) * gg_ref[...]).astype(yn_ref.dtype)
            return carry

        lax.fori_loop(0, x_ref.shape[0] // rows, body, 0)

    o_ref[...] = x_ref[...] + _mm(yn_ref[...], w_ref[...], exact)


def _out_proj(attn_parts, gm, x, g_attn, g_gmlp, w, *, tm, tn, name, exact=False):
    m, d = x.shape
    n_attn = len(attn_parts)
    wa = attn_parts[0].shape[1]
    wg = gm.shape[1]
    part = pl.BlockSpec((tm, wa), lambda i, j: (i, 0))
    return pl.pallas_call(
        functools.partial(_out_proj_kernel, n_attn=n_attn, exact=exact),
        grid=(m // tm, d // tn),
        in_specs=[part] * n_attn + [
            pl.BlockSpec((tm, wg), lambda i, j: (i, 0)),
            pl.BlockSpec((tm, tn), lambda i, j: (i, j)),
            pl.BlockSpec((1, n_attn * wa), lambda i, j: (0, 0)),
            pl.BlockSpec((1, wg), lambda i, j: (0, 0)),
            pl.BlockSpec((n_attn * wa + wg, tn), lambda i, j: (0, j)),
        ],
        out_specs=pl.BlockSpec((tm, tn), lambda i, j: (i, j)),
        out_shape=jax.ShapeDtypeStruct((m, d), F32),
        scratch_shapes=[pltpu.VMEM((tm, n_attn * wa + wg), F32 if exact else BF16)],
        compiler_params=_params("parallel", "arbitrary"),
        name=name,
    )(*attn_parts, gm, x, g_attn.reshape(1, -1), g_gmlp.reshape(1, -1), w)


ROUTE_LANES = 128


def _router_kernel(h_ref, g_ref, wr_ref, br_ref, c0_ref, m_ref, route_ref, cnt_ref, carry_ref):
    tm = h_ref.shape[0]
    n_exp = wr_ref.shape[1]

    @pl.when(pl.program_id(0) == 0)
    def _():
        carry_ref[...] = c0_ref[...]

    h = h_ref[...]
    m = h * _rms_scale(h) * g_ref[...]
    m_ref[...] = m
    logits = _mm(m, wr_ref[...], exact=True) + br_ref[...]

    lane = lax.broadcasted_iota(jnp.int32, (tm, n_exp), 1)
    work = logits
    vals, sels, idxs = [], [], []
    for _ in range(TOP_K):
        mx = jnp.max(work, axis=-1, keepdims=True)
        idx = jnp.min(jnp.where(work == mx, lane, n_exp), axis=-1, keepdims=True)
        sel = lane == idx
        vals.append(mx)
        idxs.append(idx)
        sels.append(sel)
        work = jnp.where(sel, -jnp.inf, work)
    es = [jnp.exp(v - vals[0]) for v in vals]
    inv = 1.0 / functools.reduce(lambda a, b: a + b, es)

    hot = functools.reduce(lambda a, b: a + b, [s.astype(F32) for s in sels])
    ri = lax.broadcasted_iota(jnp.int32, (tm, tm), 0)
    rj = lax.broadcasted_iota(jnp.int32, (tm, tm), 1)
    earlier = (rj < ri).astype(BF16)
    base = jnp.dot(earlier, hot.astype(BF16), preferred_element_type=F32) + carry_ref[...]
    ranks = [jnp.sum(jnp.where(s, base, 0.0), axis=-1, keepdims=True) for s in sels]
    carry_ref[...] = carry_ref[...] + jnp.sum(hot, axis=0, keepdims=True)
    cnt_ref[...] = carry_ref[...]

    ol = lax.broadcasted_iota(jnp.int32, (tm, ROUTE_LANES), 1)
    out = jnp.zeros((tm, ROUTE_LANES), F32)
    for k in range(TOP_K):
        out = jnp.where(ol == k, idxs[k].astype(F32), out)
        out = jnp.where(ol == TOP_K + k, ranks[k], out)
        out = jnp.where(ol == 2 * TOP_K + k, es[k] * inv, out)
    route_ref[...] = out


def _router(h, g, w_r, b_r, counts0, *, tm, name):
    t, d = h.shape
    n_exp = w_r.shape[1]
    return pl.pallas_call(
        _router_kernel,
        grid=(t // tm,),
        in_specs=[
            pl.BlockSpec((tm, d), lambda i: (i, 0)),
            pl.BlockSpec((1, d), lambda i: (0, 0)),
            pl.BlockSpec((d, n_exp), lambda i: (0, 0)),
            pl.BlockSpec((1, n_exp), lambda i: (0, 0)),
            pl.BlockSpec((1, n_exp), lambda i: (0, 0)),
        ],
        out_specs=[
            pl.BlockSpec((tm, d), lambda i: (i, 0)),
            pl.BlockSpec((tm, ROUTE_LANES), lambda i: (i, 0)),
            pl.BlockSpec((1, n_exp), lambda i: (0, 0)),
        ],
        out_shape=[
            jax.ShapeDtypeStruct((t, d), F32),
            jax.ShapeDtypeStruct((t, ROUTE_LANES), F32),
            jax.ShapeDtypeStruct((1, n_exp), F32),
        ],
        scratch_shapes=[pltpu.VMEM((1, n_exp), F32)],
        compiler_params=_params("arbitrary"),
        name=name,
    )(h, g.reshape(1, d), w_r, b_r.reshape(1, n_exp), counts0)


def _scatter_kernel(*refs, zero_fill, n_exp):
    if zero_fill:
        pstart_ref, pend_ref, dest_ref, m_ref, rows_hbm, zeros_ref, sem = refs
    else:
        pstart_ref, pend_ref, dest_ref, m_ref, _, rows_hbm, zeros_ref, sem = refs
    ts = dest_ref.shape[0] // TOP_K
    i = pl.program_id(0)

    if zero_fill:
        @pl.when(i == 0)
        def _():
            zeros_ref[...] = jnp.zeros_like(zeros_ref)

            def fill(e):
                return pltpu.make_async_copy(
                    zeros_ref, rows_hbm.at[pl.ds(pl.multiple_of(pend_ref[e] - MOE_CHUNK, MOE_CHUNK), MOE_CHUNK)], sem)

            def start(e, carry):
                @pl.when(pend_ref[e] > pstart_ref[e])
                def _():
                    fill(e).start()
                return carry

            def wait(e, carry):
                @pl.when(pend_ref[e] > pstart_ref[e])
                def _():
                    fill(e).wait()
                return carry

            lax.fori_loop(0, n_exp, start, 0)
            lax.fori_loop(0, n_exp, wait, 0)

    def row_copy(src, dst):
        return pltpu.make_async_copy(m_ref.at[pl.ds(src, 1)], rows_hbm.at[pl.ds(dst, 1)], sem)

    def start(t, carry):
        for k in range(TOP_K):
            row_copy(t, dest_ref[TOP_K * t + k]).start(priority=k % 2)
        return carry

    def wait(t, carry):
        for k in range(TOP_K):
            row_copy(0, 0).wait()
        return carry

    unroll = min(ts, 8)
    lax.fori_loop(0, ts, start, 0, unroll=unroll)
    lax.fori_loop(0, ts, wait, 0, unroll=unroll)


def _scatter_rows(m, dest, pstart, pend, rows_in, *, ts, n_rows, name):
    t, d = m.shape
    zero_fill = rows_in is None
    n_exp = pstart.shape[0]
    in_specs = [
        pl.BlockSpec((ts * TOP_K,), lambda i, *_: (i,), memory_space=pltpu.SMEM),
        pl.BlockSpec((ts, d), lambda i, *_: (i, 0)),
    ]
    args = [dest, m]
    aliases = {}
    if not zero_fill:
        in_specs.append(pl.BlockSpec(memory_space=pl.ANY))
        args.append(rows_in)
        aliases = {4: 0}
    return pl.pallas_call(
        functools.partial(_scatter_kernel, zero_fill=zero_fill, n_exp=n_exp),
        grid_spec=pltpu.PrefetchScalarGridSpec(
            num_scalar_prefetch=2,
            grid=(t // ts,),
            in_specs=in_specs,
            out_specs=pl.BlockSpec(memory_space=pl.ANY),
            scratch_shapes=[pltpu.VMEM((MOE_CHUNK, d), F32), pltpu.SemaphoreType.DMA(())],
        ),
        out_shape=jax.ShapeDtypeStruct((n_rows, d), F32),
        input_output_aliases=aliases,
        compiler_params=_params("arbitrary"),
        name=name,
    )(pstart, pend, *args)


def _expert_kernel(ir_ref, inch_ref, ie_ref, nv_ref,
                   rows_hbm, wgu_hbm, wd_hbm, bg_ref, bl_ref, bd_ref, ys_hbm,
                   stage, xb, hdn, ybuf, wring, sem_in, sem_out, sem_w, *, n_ff, n_dt):
    i = pl.program_id(0)
    s = pl.program_id(1)
    n_items = pl.num_programs(0)
    n_steps = n_ff + n_dt // 2
    nch = inch_ref[i]
    tile = ybuf.shape[2]
    d_ff = n_ff * tile
    step_id = i * n_steps + s
    live_steps = nv_ref[0] * n_steps
    slot = lax.rem(step_id, WEIGHT_RING)

    def weight_copies(item, step, ring_slot, first_phase):
        e = ie_ref[item]

        def tile_copy(src, col, k):
            return pltpu.make_async_copy(src.at[e, :, pl.ds(pl.multiple_of(col, tile), tile)],
                                         wring.at[ring_slot, k], sem_w.at[ring_slot, k])

        if first_phase:
            return [tile_copy(wgu_hbm, step * tile, 0), tile_copy(wgu_hbm, d_ff + step * tile, 1)]
        col = (step - n_ff) * (2 * tile)
        return [tile_copy(wd_hbm, col, 0), tile_copy(wd_hbm, col + tile, 1)]

    def weights(op, item, step, ring_slot):
        @pl.when(step < n_ff)
        def _():
            for cp in weight_copies(item, step, ring_slot, True):
                op(cp)

        @pl.when(step >= n_ff)
        def _():
            for cp in weight_copies(item, step, ring_slot, False):
                op(cp)

    def fetch(ahead):
        tgt = step_id + ahead

        @pl.when(tgt < live_steps)
        def _():
            weights(lambda cp: cp.start(), lax.div(tgt, n_steps), lax.rem(tgt, n_steps),
                    lax.rem(tgt, WEIGHT_RING))

    @pl.when(step_id == 0)
    def _():
        for ahead in range(WEIGHT_RING - 1):
            fetch(ahead)

    fetch(WEIGHT_RING - 1)

    def chunk(c):
        return slice(c * MOE_CHUNK, (c + 1) * MOE_CHUNK)

    def row_window(item, c):
        return pl.ds(pl.multiple_of(ir_ref[item], MOE_CHUNK) + c * MOE_CHUNK, MOE_CHUNK)

    def load(item, c):
        return pltpu.make_async_copy(rows_hbm.at[row_window(item, c)], stage.at[chunk(c)], sem_in.at[c])

    def store(item, c, n):
        return pltpu.make_async_copy(
            ybuf.at[n, chunk(c)], ys_hbm.at[row_window(item, c), pl.ds(n * tile, tile)], sem_out.at[c])

    def for_chunks(count, fn):
        for c in range(MOE_MAX_CHUNKS):
            pl.when(c < count)(functools.partial(fn, c))

    def for_sizes(fn):
        for nc in range(1, MOE_MAX_CHUNKS + 1):
            pl.when(nch == nc)(functools.partial(fn, nc * MOE_CHUNK))

    @pl.when(s == 0)
    def _():
        @pl.when(i == 0)
        def _():
            for_chunks(nch, lambda c: load(0, c).start())

        def land(c):
            load(i, c).wait()
            xb[chunk(c), :] = stage[chunk(c), :].astype(BF16)

        for_chunks(nch, land)

    @pl.when(nch > 0)
    def _():
        weights(lambda cp: cp.wait(), i, s, slot)

    @pl.when((s < n_ff) & (nch > 0))
    def _():
        def phase1(rows):
            wg = wring[slot, 0].astype(BF16)
            wl = wring[slot, 1].astype(BF16)
            x = xb[:rows, :]
            gate = jnp.dot(x, wg, preferred_element_type=F32) + bg_ref[...]
            lin = jnp.dot(x, wl, preferred_element_type=F32) + bl_ref[...]
            gate = jnp.minimum(gate, SWIGLU_LIMIT)
            lin = jnp.clip(lin, -SWIGLU_LIMIT, SWIGLU_LIMIT)
            act = gate * jax.nn.sigmoid(SWIGLU_ALPHA * gate) * (lin + 1.0)
            hdn[s, :rows, :] = act.astype(BF16)

        for_sizes(phase1)

    @pl.when(s == n_ff)
    def _():
        @pl.when(i > 0)
        def _():
            def drain(c):
                for n in range(n_dt):
                    store(i - 1, c, n).wait()

            for_chunks(inch_ref[jnp.maximum(i - 1, 0)], drain)

        @pl.when(i + 1 < n_items)
        def _():
            nxt = jnp.minimum(i + 1, n_items - 1)
            for_chunks(inch_ref[nxt], lambda c: load(nxt, c).start())

    @pl.when((s >= n_ff) & (nch > 0))
    def _():
        n = s - n_ff

        def phase2(rows):
            for k in range(2):
                wd = wring[slot, k].astype(BF16)
                y = bd_ref[:, k * tile:(k + 1) * tile]
                for f in range(n_ff):
                    y = y + jnp.dot(hdn[f, :rows, :], wd[f * tile:(f + 1) * tile, :],
                                    preferred_element_type=F32)
                ybuf[2 * n + k, :rows, :] = y

        for_sizes(phase2)

    @pl.when(s == n_steps - 1)
    def _():
        def flush(c):
            for n in range(n_dt):
                store(i, c, n).start()

        for_chunks(nch, flush)

        @pl.when(i == n_items - 1)
        def _():
            def drain(c):
                for n in range(n_dt):
                    store(i, c, n).wait()

            for_chunks(nch, drain)


def _expert_ffn(rows, w_gu, b_gu, w_down, b_down, item_e, item_row0, item_nch):
    n_rows, d = rows.shape
    n_exp, _, two_ff = w_gu.shape
    d_ff = two_ff // 2
    tile = MOE_FF_TILE
    n_ff = d_ff // tile
    n_dt = d // tile
    n_steps = n_ff + n_dt // 2
    n_items = item_e.shape[0]
    r_max = MOE_MAX_CHUNKS * MOE_CHUNK

    assert d == d_ff, "one ring slot shape serves both phases' weight tiles"
    n_live = jnp.sum((item_nch > 0).astype(jnp.int32)).reshape(1)

    def gate_map(i, s, ir, inch, ie, nv):
        return ie[i], 0, jnp.minimum(s, n_ff - 1)

    def lin_map(i, s, ir, inch, ie, nv):
        return ie[i], 0, n_ff + jnp.minimum(s, n_ff - 1)

    def down_map(i, s, ir, inch, ie, nv):
        return ie[i], 0, jnp.maximum(s - n_ff, 0)

    return pl.pallas_call(
        functools.partial(_expert_kernel, n_ff=n_ff, n_dt=n_dt),
        grid_spec=pltpu.PrefetchScalarGridSpec(
            num_scalar_prefetch=4,
            grid=(n_items, n_steps),
            in_specs=[
                pl.BlockSpec(memory_space=pl.ANY),
                pl.BlockSpec(memory_space=pl.ANY),
                pl.BlockSpec(memory_space=pl.ANY),
                pl.BlockSpec((None, 1, tile), gate_map),
                pl.BlockSpec((None, 1, tile), lin_map),
                pl.BlockSpec((None, 1, 2 * tile), down_map),
            ],
            out_specs=pl.BlockSpec(memory_space=pl.ANY),
            scratch_shapes=[
                pltpu.VMEM((r_max, d), F32),
                pltpu.VMEM((r_max, d), BF16),
                pltpu.VMEM((n_ff, r_max, tile), BF16),
                pltpu.VMEM((n_dt, r_max, tile), F32),
                pltpu.VMEM((WEIGHT_RING, 2, d, tile), F32),
                pltpu.SemaphoreType.DMA((MOE_MAX_CHUNKS,)),
                pltpu.SemaphoreType.DMA((MOE_MAX_CHUNKS,)),
                pltpu.SemaphoreType.DMA((WEIGHT_RING, 2)),
            ],
        ),
        out_shape=jax.ShapeDtypeStruct((n_rows, d), F32),
        compiler_params=_params("arbitrary", "arbitrary"),
        name="expert_ffn",
    )(item_row0, item_nch, item_e.astype(jnp.int32), n_live, rows, w_gu, w_down,
      b_gu.reshape(n_exp, 1, two_ff), b_gu.reshape(n_exp, 1, two_ff),
      b_down.reshape(n_exp, 1, d))


def _combine_kernel(dest_ref, route_ref, h_ref, ys_hbm, o_ref, buf, sem):
    tg = h_ref.shape[0]

    def row_copy(src, k, t):
        return pltpu.make_async_copy(ys_hbm.at[pl.ds(src, 1)], buf.at[k, pl.ds(t, 1)], sem)

    def start(t, carry):
        for k in range(TOP_K):
            row_copy(dest_ref[TOP_K * t + k], k, t).start(priority=k % 2)
        return carry

    def wait(t, carry):
        for k in range(TOP_K):
            row_copy(0, k, 0).wait()
        return carry

    unroll = min(tg, 8)
    lax.fori_loop(0, tg, start, 0, unroll=unroll)
    lax.fori_loop(0, tg, wait, 0, unroll=unroll)
    out = h_ref[...]
    for k in range(TOP_K):
        out = out + route_ref[:, 2 * TOP_K + k:2 * TOP_K + k + 1] * buf[k]
    o_ref[...] = out


def _combine(dest, route, h, ys, *, tg, name):
    t, d = h.shape
    return pl.pallas_call(
        _combine_kernel,
        grid=(t // tg,),
        in_specs=[
            pl.BlockSpec((tg * TOP_K,), lambda i: (i,), memory_space=pltpu.SMEM),
            pl.BlockSpec((tg, ROUTE_LANES), lambda i: (i, 0)),
            pl.BlockSpec((tg, d), lambda i: (i, 0)),
            pl.BlockSpec(memory_space=pl.ANY),
        ],
        out_specs=pl.BlockSpec((tg, d), lambda i: (i, 0)),
        out_shape=jax.ShapeDtypeStruct((t, d), F32),
        scratch_shapes=[pltpu.VMEM((TOP_K, tg, d), F32), pltpu.SemaphoreType.DMA(())],
        compiler_params=_params("arbitrary"),
        name=name,
    )(dest, route, h, ys)


def _ple_kernel(h_ref, p_ref, g_ref, wg_ref, wp_ref, o_ref, hn_ref):
    rows = min(h_ref.shape[0], 128)

    def body(c, carry):
        sl = pl.ds(pl.multiple_of(c * rows, rows), rows)
        h = h_ref[sl, :]
        hn_ref[sl, :] = (h * _rms_scale(h) * g_ref[...]).astype(BF16)
        return carry

    lax.fori_loop(0, h_ref.shape[0] // rows, body, 0)
    gate = jax.nn.sigmoid(jnp.dot(hn_ref[...], wg_ref[...].astype(BF16), preferred_element_type=F32))
    proj = jnp.dot(p_ref[...].astype(BF16), wp_ref[...].astype(BF16), preferred_element_type=F32)
    o_ref[...] = h_ref[...] + gate * proj


def _ple(h, p, g, w_gate, w_proj, *, tm, name):
    t, d = h.shape
    pd = p.shape[1]
    return pl.pallas_call(
        _ple_kernel,
        grid=(t // tm,),
        in_specs=[
            pl.BlockSpec((tm, d), lambda i: (i, 0)),
            pl.BlockSpec((tm, pd), lambda i: (i, 0)),
            pl.BlockSpec((1, d), lambda i: (0, 0)),
            pl.BlockSpec((d, d), lambda i: (0, 0)),
            pl.BlockSpec((pd, d), lambda i: (0, 0)),
        ],
        out_specs=pl.BlockSpec((tm, d), lambda i: (i, 0)),
        out_shape=jax.ShapeDtypeStruct((t, d), F32),
        scratch_shapes=[pltpu.VMEM((tm, d), BF16)],
        compiler_params=_params("parallel"),
        name=name,
    )(h, p, g.reshape(1, d), w_gate, w_proj)


def _decode_kernel(*refs, groups, hpg):
    ng = len(groups)
    z_ref = refs[0]
    caches = refs[1:1 + ng]
    (cos_ref, sa_ref, sb_ref, gq_ref, gk_ref, lng_ref, lnb_ref, ws0_ref, bs0_ref,
     attn_ref, kn_ref, gm_ref, vn_ref) = refs[1 + ng:]
    heads = ng * hpg
    aw = heads * HEAD_DIM
    scale = HEAD_DIM ** -0.5
    tabs = (cos_ref[...], sa_ref[...], sb_ref[...])

    def head(base, hh):
        return z_ref[:, base + hh * HEAD_DIM:base + (hh + 1) * HEAD_DIM]

    outs, lses = [], []
    for g in range(ng):
        c_ref = caches[g]
        for j in range(hpg):
            hh = g * hpg + j
            qn = _norm_rope(head(0, hh), gq_ref[...], *tabs) * scale
            kn = _norm_rope(head(aw, hh), gk_ref[...], *tabs)
            vv = head(2 * aw, hh)
            kn_ref[:, hh * HEAD_DIM:(hh + 1) * HEAD_DIM] = kn
            kc = c_ref[:, 0, j, :]
            vc = c_ref[:, 1, j, :]
            s_c = jnp.sum(kc * qn, axis=-1, keepdims=True)
            s_n = jnp.sum(kn * qn, axis=-1, keepdims=True)
            m = jnp.maximum(jnp.max(s_c, axis=0, keepdims=True), s_n)
            p_c = jnp.exp(s_c - m)
            p_n = jnp.exp(s_n - m)
            den = jnp.sum(p_c, axis=0, keepdims=True) + p_n
            o = (jnp.sum(p_c * vc, axis=0, keepdims=True) + p_n * vv) / den
            outs.append(o)
            lses.append(m + jnp.log(den))

    for j in range(hpg):
        ls = [lses[g * hpg + j] for g in range(ng)]
        mx = functools.reduce(jnp.maximum, ls)
        es = [jnp.exp(l - mx) for l in ls]
        inv = 1.0 / functools.reduce(lambda a, b: a + b, es)
        for g in range(ng):
            hh = g * hpg + j
            attn_ref[:, hh * HEAD_DIM:(hh + 1) * HEAD_DIM] = outs[hh] * (es[g] * inv)

    gw = gm_ref.shape[1]
    for g in range(gw // GMLP_CH):
        cols = slice(g * GMLP_CH, (g + 1) * GMLP_CH)
        v = _gelu(z_ref[:, 3 * aw + gw + g * GMLP_CH:3 * aw + gw + (g + 1) * GMLP_CH])
        mu = jnp.mean(v, axis=-1, keepdims=True)
        vc = v - mu
        var = jnp.mean(vc * vc, axis=-1, keepdims=True)
        vn = vc * lax.rsqrt(var + EPS) * lng_ref[:, cols] + lnb_ref[:, cols]
        vn_ref[:, cols] = vn
        u = _gelu(z_ref[:, 3 * aw + g * GMLP_CH:3 * aw + (g + 1) * GMLP_CH])
        gm_ref[:, cols] = u * (ws0_ref[:, cols] * vn + bs0_ref[:, cols])


def _sample_mixers(zs, caches, tables, g_q, g_k, ln_g, ln_b, w_s, b_s, *, hpg):
    bsz, in_width = zs.shape
    ng = len(DILATION_GROUPS)
    aw = ng * hpg * HEAD_DIM
    gw = (in_width - 3 * aw) // 2
    in_specs = [pl.BlockSpec((None, 1, in_width), lambda b: (b, 0, 0))]
    cache_args = []
    for (window, dil), c in zip(DILATION_GROUPS, caches):
        wc = c.shape[-4]
        assert wc == window and c.shape[-5] == bsz, "window caches must hold a full window"
        steps = wc // dil
        cache_args.append(c.reshape(bsz, steps, dil, 2, hpg, HEAD_DIM))
        in_specs.append(pl.BlockSpec((None, steps, None, 2, hpg, HEAD_DIM),
                                     lambda b: (b, 0, 0, 0, 0, 0)))
    vec = pl.BlockSpec((1, HEAD_DIM), lambda b: (0, 0))
    wide = pl.BlockSpec((1, gw), lambda b: (0, 0))
    in_specs += [vec] * 5 + [wide] * 4

    def out(w):
        return pl.BlockSpec((None, 1, w), lambda b: (b, 0, 0)), jax.ShapeDtypeStruct((bsz, 1, w), F32)

    specs, shapes = zip(out(aw), out(aw), out(gw), out(gw))
    groups = w_s.shape[0]
    ws0 = jnp.repeat(w_s[:, 0, 0], GMLP_CH).reshape(1, gw)
    bs0 = jnp.repeat(b_s[:, 0], GMLP_CH).reshape(1, gw)
    assert groups * GMLP_CH == gw
    res = pl.pallas_call(
        functools.partial(_decode_kernel, groups=DILATION_GROUPS, hpg=hpg),
        grid=(bsz,),
        in_specs=in_specs,
        out_specs=list(specs),
        out_shape=list(shapes),
        compiler_params=_params("parallel"),
        name="sample_mixers",
    )(zs.reshape(bsz, 1, in_width), *cache_args, *tables,
      g_q.reshape(1, HEAD_DIM), g_k.reshape(1, HEAD_DIM),
      ln_g.reshape(1, gw), ln_b.reshape(1, gw), ws0, bs0)
    return [r.reshape(bsz, -1) for r in res]


def _expert_layout(counts, n_items):
    counts = counts.astype(jnp.int32)
    padded = (counts + MOE_CHUNK - 1) // MOE_CHUNK * MOE_CHUNK
    pend = jnp.cumsum(padded)
    pstart = pend - padded
    r_max = MOE_MAX_CHUNKS * MOE_CHUNK
    per_e = (padded + r_max - 1) // r_max
    item_end = jnp.cumsum(per_e)
    total = item_end[-1]
    ids = jnp.arange(n_items, dtype=jnp.int32)
    clamped = jnp.minimum(ids, total - 1)
    e = jnp.sum((item_end[None, :] <= clamped[:, None]).astype(jnp.int32), axis=1)
    local = clamped - (item_end[e] - per_e[e])
    row0 = pstart[e] + local * r_max
    nch = jnp.minimum(padded[e] - local * r_max, r_max) // MOE_CHUNK
    nch = jnp.where(ids < total, nch, 0)
    return pstart, pend, e, row0.astype(jnp.int32), nch.astype(jnp.int32)


def _destinations(route, pstart):
    e = route[:, :TOP_K].astype(jnp.int32)
    rank = route[:, TOP_K:2 * TOP_K].astype(jnp.int32)
    return (pstart[e] + rank).reshape(-1)


def kernel(x_prompt, x_sample, cache_kv_w128, cache_kv_w512, cache_kv_w2048, p_prompt, p_sample,
           g_mix, w_in, g_q, g_k, gmlp_ln_g, gmlp_ln_b, gmlp_w_s, gmlp_b_s, g_out_attn, g_out_gmlp,
           w_out, g_ffn, w_router, b_router, w_gu, b_gu, w_down, b_down, g_ple, w_ple_gate,
           w_ple_proj):
    depth = g_mix.shape[0]
    assert depth == 1, "single-layer step"
    batch, seq, d = x_prompt.shape
    dec_batch, dec_seq, _ = x_sample.shape
    assert dec_seq == 1
    in_width = w_in.shape[2]
    gw = gmlp_ln_g.shape[1] * gmlp_ln_g.shape[2]
    aw = (in_width - 2 * gw) // 3
    ng = len(DILATION_GROUPS)
    hpg = aw // HEAD_DIM // ng
    n_exp = w_router.shape[2]
    tp = batch * seq
    ts = dec_batch * dec_seq

    def layer(a):
        return a.reshape(a.shape[1:])

    caches = (cache_kv_w128, cache_kv_w512, cache_kv_w2048)
    (p_prompt, p_sample, g_mix, w_in, g_q, g_k, gmlp_ln_g, gmlp_ln_b, gmlp_w_s, gmlp_b_s,
     g_out_attn, g_out_gmlp, w_out, g_ffn, w_router, b_router, w_gu, b_gu, w_down, b_down, g_ple,
     w_ple_gate, w_ple_proj) = [layer(a) for a in (
         p_prompt, p_sample, g_mix, w_in, g_q, g_k, gmlp_ln_g, gmlp_ln_b, gmlp_w_s, gmlp_b_s,
         g_out_attn, g_out_gmlp, w_out, g_ffn, w_router, b_router, w_gu, b_gu, w_down, b_down,
         g_ple, w_ple_gate, w_ple_proj)]

    w_in_b, w_out_b = w_in.astype(BF16), w_out.astype(BF16)
    w_ple_gate_b, w_ple_proj_b = w_ple_gate.astype(BF16), w_ple_proj.astype(BF16)
    xp = x_prompt.reshape(tp, d)
    z = _norm_matmul(xp, g_mix, w_in_b, tm=512, tn=in_width // 2, name="prompt_in_proj",
                     rows_inner=True)
    z3 = z.reshape(batch, seq, in_width)
    tables_p = _rope_tables(jnp.arange(seq, dtype=jnp.int32))
    attn_p, kn_p = _prompt_attention(z3, g_q, g_k, tables_p, batch=batch, seq=seq, hpg=hpg)
    gm_p = _prompt_gmlp(z3, gmlp_ln_g, gmlp_ln_b, gmlp_w_s, gmlp_b_s,
                        batch=batch, seq=seq, width=gw, col_block=3 * aw // gw)
    hp = _out_proj([a.reshape(tp, -1) for a in attn_p], gm_p.reshape(tp, gw), xp,
                   g_out_attn, g_out_gmlp, w_out_b, tm=512, tn=d, name="prompt_out_proj")

    xs = x_sample.reshape(ts, d)
    zs = _norm_matmul(xs, g_mix, w_in, tm=ts, tn=512, name="sample_in_proj", exact=True)
    tables_s = _rope_tables(PAST_LEN + jnp.arange(dec_seq, dtype=jnp.int32))
    attn_s, kn_s, gm_s, vn_s = _sample_mixers(
        zs, caches, tables_s, g_q, g_k, gmlp_ln_g, gmlp_ln_b, gmlp_w_s, gmlp_b_s, hpg=hpg)
    gwid = hpg * HEAD_DIM
    hs = _out_proj([attn_s[:, g * gwid:(g + 1) * gwid] for g in range(ng)], gm_s, xs,
                   g_out_attn, g_out_gmlp, w_out, tm=ts, tn=512, name="sample_out_proj",
                   exact=True)

    zero_counts = jnp.zeros((1, n_exp), F32)
    m_p, route_p, counts_p = _router(hp, g_ffn, w_router, b_router, zero_counts,
                                     tm=512, name="prompt_router")
    m_s, route_s, counts = _router(hs, g_ffn, w_router, b_router, counts_p,
                                   tm=ts, name="sample_router")
    n_assign = (tp + ts) * TOP_K
    n_chunks = -(-n_assign // MOE_CHUNK) + n_exp
    n_rows = n_chunks * MOE_CHUNK
    n_items = n_exp + -(-n_chunks // MOE_MAX_CHUNKS)
    pstart, pend, item_e, item_row0, item_nch = _expert_layout(counts.reshape(n_exp), n_items)
    dest_p = _destinations(route_p, pstart)
    dest_s = _destinations(route_s, pstart)
    rows = _scatter_rows(m_p, dest_p, pstart, pend, None, ts=512, n_rows=n_rows,
                         name="prompt_scatter")
    rows = _scatter_rows(m_s, dest_s, pstart, pend, rows, ts=ts, n_rows=n_rows,
                         name="sample_scatter")
    ys = _expert_ffn(rows, w_gu, b_gu, w_down, b_down, item_e, item_row0, item_nch)
    hp = _combine(dest_p, route_p, hp, ys, tg=256, name="prompt_combine")
    hs = _combine(dest_s, route_s, hs, ys, tg=ts, name="sample_combine")

    yp = _ple(hp, p_prompt.reshape(tp, -1), g_ple, w_ple_gate_b, w_ple_proj_b,
              tm=512, name="prompt_ple")
    ysm = _ple(hs, p_sample.reshape(ts, -1), g_ple, w_ple_gate_b, w_ple_proj_b,
               tm=ts, name="sample_ple")

    v0 = 2 * aw
    kv_prompt, kv_sample = [], []
    for g, (window, _) in enumerate(DILATION_GROUPS):
        wc = min(window, seq)
        kg = kn_p[g][:, seq - wc:].reshape(batch, wc, hpg, HEAD_DIM)
        vg = z3[:, seq - wc:, v0 + g * gwid:v0 + (g + 1) * gwid].reshape(batch, wc, hpg, HEAD_DIM)
        kv_prompt.append(jnp.stack([kg, vg], axis=2)[None])
        ksg = kn_s[:, g * gwid:(g + 1) * gwid].reshape(dec_batch, dec_seq, hpg, HEAD_DIM)
        vsg = zs[:, v0 + g * gwid:v0 + (g + 1) * gwid].reshape(dec_batch, dec_seq, hpg, HEAD_DIM)
        kv_sample.append(jnp.stack([ksg, vsg], axis=2)[None])
    gmlp_v = vn_s.reshape(1, dec_batch, dec_seq, gw // GMLP_CH, GMLP_CH)
    return (yp.reshape(batch, seq, d), ysm.reshape(dec_batch, dec_seq, d),
            *kv_prompt, *kv_sample, gmlp_v)
```

```python
import functools

import jax
import jax.numpy as jnp
from jax import lax
from jax.experimental import pallas as pl
from jax.experimental.pallas import tpu as pltpu

F32 = jnp.float32
BF16 = jnp.bfloat16

EPS = 1e-6
HEAD_DIM = 128
ROT_DIM = HEAD_DIM // 4
ROPE_THETA = 500000.0
DILATION_GROUPS = ((128, 1), (512, 4), (2048, 16))
PAST_LEN = 16384
QBLK = 128
CHUNK = 128
GMLP_CH = 128
TOP_K = 4
SWIGLU_LIMIT = 7.0
SWIGLU_ALPHA = 1.702

V7X_VMEM_BYTES = 64 * 1024 * 1024
VMEM_LIMIT = V7X_VMEM_BYTES - 8 * 1024 * 1024

MOE_CHUNK = 256
MOE_MAX_CHUNKS = 5
MOE_FF_TILE = 256
MOE_P1_TILES = 2
MOE_P2_TILES = 4
WEIGHT_RING = 2


def _params(*sem):
    return pltpu.CompilerParams(dimension_semantics=sem, vmem_limit_bytes=VMEM_LIMIT)


def _split_bf16(a):
    hi = a.astype(BF16)
    lo = (a - hi.astype(F32)).astype(BF16)
    return hi, lo


def _mm(a, b, exact=False):
    d = functools.partial(jnp.dot, preferred_element_type=F32)
    if not exact:
        return d(a.astype(BF16), b.astype(BF16))
    ah, al = _split_bf16(a)
    bh, bl = _split_bf16(b)
    return d(ah, bh) + (d(ah, bl) + d(al, bh))


def _gelu(x):
    return 0.5 * x * (1.0 + jnp.tanh(0.7978845608028654 * (x + 0.044715 * (x * x * x))))


def _rms_scale(x):
    return lax.rsqrt(jnp.mean(x * x, axis=-1, keepdims=True) + EPS)


def _norm_matmul_kernel(x_ref, g_ref, w_ref, o_ref, xn_ref, *, exact, rows_inner):
    def normalise():
        rows = min(x_ref.shape[0], 128)

        def body(c, carry):
            sl = pl.ds(pl.multiple_of(c * rows, rows), rows)
            x = x_ref[sl, :]
            xn_ref[sl, :] = (x * _rms_scale(x) * g_ref[...]).astype(xn_ref.dtype)
            return carry

        lax.fori_loop(0, x_ref.shape[0] // rows, body, 0)

    if rows_inner:
        normalise()
    else:
        pl.when(pl.program_id(1) == 0)(normalise)

    o_ref[...] = _mm(xn_ref[...], w_ref[...], exact)


def _norm_matmul(x, g, w, *, tm, tn, name, exact=False, rows_inner=False):
    m, k = x.shape
    n = w.shape[1]
    if rows_inner:
        grid = (n // tn, m // tm)
        row, col = (lambda j, i: i), (lambda j, i: j)
    else:
        grid = (m // tm, n // tn)
        row, col = (lambda i, j: i), (lambda i, j: j)
    return pl.pallas_call(
        functools.partial(_norm_matmul_kernel, exact=exact, rows_inner=rows_inner),
        grid=grid,
        in_specs=[
            pl.BlockSpec((tm, k), lambda a, b: (row(a, b), 0)),
            pl.BlockSpec((1, k), lambda a, b: (0, 0)),
            pl.BlockSpec((k, tn), lambda a, b: (0, col(a, b))),
        ],
        out_specs=pl.BlockSpec((tm, tn), lambda a, b: (row(a, b), col(a, b))),
        out_shape=jax.ShapeDtypeStruct((m, n), F32),
        scratch_shapes=[pltpu.VMEM((tm, k), F32 if exact else BF16)],
        compiler_params=_params("parallel", "arbitrary"),
        name=name,
    )(x, g.reshape(1, k), w)


def _rope_tables(pos):
    half = ROT_DIM // 2
    inv = 1.0 / (ROPE_THETA ** (jnp.arange(half, dtype=F32) * (2.0 / ROT_DIM)))
    ang = pos.astype(F32)[:, None] * inv[None, :]
    cos, sin = jnp.cos(ang), jnp.sin(ang)
    t = pos.shape[0]
    pad = HEAD_DIM - ROT_DIM
    cos_t = jnp.concatenate([cos, cos, jnp.ones((t, pad), F32)], axis=1)
    sa = jnp.concatenate([-sin, jnp.zeros((t, HEAD_DIM - half), F32)], axis=1)
    sb = jnp.concatenate([jnp.zeros((t, half), F32), sin, jnp.zeros((t, pad), F32)], axis=1)
    return cos_t, sa, sb


def _norm_rope(x, gain, cos_t, sa, sb):
    half = ROT_DIM // 2
    xn = x * _rms_scale(x) * gain
    up = pltpu.roll(xn, HEAD_DIM - half, 1)
    dn = pltpu.roll(xn, half, 1)
    return xn * cos_t + up * sa + dn * sb


def _band_block(q, k, v, mask):
    s = lax.dot_general(q.astype(BF16), k.astype(BF16), (((1,), (1,)), ((), ())),
                        preferred_element_type=F32)
    s = jnp.where(mask, s, -jnp.inf)
    m = jnp.max(s, axis=-1, keepdims=True)
    p = jnp.exp(s - m)
    den = jnp.sum(p, axis=-1, keepdims=True)
    o = jnp.dot((p / den).astype(BF16), v.astype(BF16), preferred_element_type=F32)
    return o, m + jnp.log(den)


def _attn_kernel(*refs, seq, groups):
    ng = len(groups)
    qkv = refs[:3 * ng]
    cos_ref, sa_ref, sb_ref, gq_ref, gk_ref = refs[3 * ng:3 * ng + 5]
    outs = refs[3 * ng + 5:3 * ng + 5 + 2 * ng]
    o_refs, kn_refs = outs[:ng], outs[ng:]
    qs, ks = refs[3 * ng + 5 + 2 * ng:3 * ng + 7 + 2 * ng]
    ls_refs = refs[3 * ng + 7 + 2 * ng:]
    scale = HEAD_DIM ** -0.5
    nrm_rows = 256

    qi = lax.broadcasted_iota(jnp.int32, (QBLK, 2 * QBLK), 0)
    kj = lax.broadcasted_iota(jnp.int32, (QBLK, 2 * QBLK), 1)
    qi1 = lax.broadcasted_iota(jnp.int32, (QBLK, QBLK), 0)
    kj1 = lax.broadcasted_iota(jnp.int32, (QBLK, QBLK), 1)

    for g, (window, dil) in enumerate(groups):
        q_ref, k_ref, v_ref = qkv[3 * g:3 * g + 3]
        o_ref, kn_ref, ls_ref = o_refs[g], kn_refs[g], ls_refs[g]
        n_steps = window // dil
        nb = seq // dil // QBLK

        def prep(c, carry, q_ref=q_ref, k_ref=k_ref, kn_ref=kn_ref):
            sl = pl.ds(pl.multiple_of(c * nrm_rows, nrm_rows), nrm_rows)
            tabs = (cos_ref[sl, :], sa_ref[sl, :], sb_ref[sl, :])
            qs[sl, :] = _norm_rope(q_ref[sl, :], gq_ref[...], *tabs) * scale
            kn = _norm_rope(k_ref[sl, :], gk_ref[...], *tabs)
            kn_ref[sl, :] = kn
            ks[sl, :] = kn
            return carry

        lax.fori_loop(0, seq // nrm_rows, prep, 0, unroll=2)

        def rows(start, size, dil=dil):
            return pl.ds(start, size) if dil == 1 else pl.ds(start, size, stride=dil)

        band_mask = (kj >= qi + (QBLK - n_steps)) & (kj <= qi + QBLK)
        first_mask = (kj1 <= qi1) & (qi1 - kj1 <= n_steps)

        def one_class(r, carry, v_ref=v_ref, o_ref=o_ref, ls_ref=ls_ref, rows=rows,
                      band_mask=band_mask, first_mask=first_mask, dil=dil, nb=nb):
            sl0 = rows(r, QBLK)
            o, lse = _band_block(qs[sl0, :], ks[sl0, :], v_ref[sl0, :], first_mask)
            o_ref[sl0, :] = o
            ls_ref[sl0, :] = jnp.broadcast_to(lse, (QBLK, HEAD_DIM))

            def one_block(lb, carry2):
                q0 = r + dil * QBLK * lb
                k0 = q0 - dil * QBLK
                slq, slk = rows(q0, QBLK), rows(k0, 2 * QBLK)
                o, lse = _band_block(qs[slq, :], ks[slk, :], v_ref[slk, :], band_mask)
                o_ref[slq, :] = o
                ls_ref[slq, :] = jnp.broadcast_to(lse, (QBLK, HEAD_DIM))
                return carry2

            lax.fori_loop(1, nb, one_block, 0, unroll=min(5, max(nb - 1, 1)))
            return carry

        lax.fori_loop(0, dil, one_class, 0, unroll=min(8, dil) if nb == 1 else 1)

    def merge(c, carry):
        sl = pl.ds(pl.multiple_of(c * nrm_rows, nrm_rows), nrm_rows)
        ls = [ls_refs[g][sl, :] for g in range(ng)]
        mx = functools.reduce(jnp.maximum, ls)
        es = [jnp.exp(l - mx) for l in ls]
        inv = 1.0 / functools.reduce(lambda a, b: a + b, es)
        for g in range(ng):
            o_refs[g][sl, :] = o_refs[g][sl, :] * (es[g] * inv)
        return carry

    lax.fori_loop(0, seq // nrm_rows, merge, 0)


def _prompt_attention(z, g_q, g_k, tables, *, batch, seq, hpg):
    ng = len(DILATION_GROUPS)
    heads = ng * hpg
    blk = (None, seq, HEAD_DIM)

    def col(c):
        return pl.BlockSpec(blk, lambda b, h, c=c: (b, 0, c + h))

    in_specs = []
    for g in range(ng):
        in_specs += [col(g * hpg), col(heads + g * hpg), col(2 * heads + g * hpg)]
    tab = pl.BlockSpec((seq, HEAD_DIM), lambda b, h: (0, 0))
    vec = pl.BlockSpec((1, HEAD_DIM), lambda b, h: (0, 0))
    in_specs += [tab, tab, tab, vec, vec]
    out_spec = pl.BlockSpec(blk, lambda b, h: (b, 0, h))
    out_sds = jax.ShapeDtypeStruct((batch, seq, hpg * HEAD_DIM), F32)
    res = pl.pallas_call(
        functools.partial(_attn_kernel, seq=seq, groups=DILATION_GROUPS),
        grid=(batch, hpg),
        in_specs=in_specs,
        out_specs=[out_spec] * (2 * ng),
        out_shape=[out_sds] * (2 * ng),
        scratch_shapes=[pltpu.VMEM((seq, HEAD_DIM), F32)] * (2 + ng),
        compiler_params=_params("parallel", "arbitrary"),
        name="prompt_attention",
    )(*([z] * (3 * ng)), *tables, g_q.reshape(1, HEAD_DIM), g_k.reshape(1, HEAD_DIM))
    return res[:ng], res[ng:]


def _gmlp_kernel(u_ref, vg_ref, lng_ref, lnb_ref, ws_ref, bs_ref, o_ref):
    groups = ws_ref.shape[0]
    ci = lax.broadcasted_iota(jnp.int32, (CHUNK, CHUNK), 0)
    cj = lax.broadcasted_iota(jnp.int32, (CHUNK, CHUNK), 1)
    for g in range(groups):
        cols = slice(g * GMLP_CH, (g + 1) * GMLP_CH)
        ws = jnp.where(cj <= ci, ws_ref[g], 0.0).astype(BF16)
        bias = bs_ref[:, g:g + 1]
        for n in range(u_ref.shape[0] // CHUNK):
            rows = slice(n * CHUNK, (n + 1) * CHUNK)
            v = _gelu(vg_ref[rows, cols])
            mu = jnp.mean(v, axis=-1, keepdims=True)
            vc = v - mu
            var = jnp.mean(vc * vc, axis=-1, keepdims=True)
            vn = vc * lax.rsqrt(var + EPS) * lng_ref[:, cols] + lnb_ref[:, cols]
            mixed = jnp.dot(ws, vn.astype(BF16), preferred_element_type=F32) + bias
            o_ref[rows, cols] = _gelu(u_ref[rows, cols]) * mixed


def _prompt_gmlp(z, ln_g, ln_b, w_s, b_s, *, batch, seq, width, col_block):
    tc = 4 * CHUNK
    groups = w_s.shape[0]
    return pl.pallas_call(
        _gmlp_kernel,
        grid=(batch, seq // tc),
        in_specs=[
            pl.BlockSpec((None, tc, width), lambda b, n: (b, n, col_block)),
            pl.BlockSpec((None, tc, width), lambda b, n: (b, n, col_block + 1)),
            pl.BlockSpec((1, width), lambda b, n: (0, 0)),
            pl.BlockSpec((1, width), lambda b, n: (0, 0)),
            pl.BlockSpec((groups, CHUNK, CHUNK), lambda b, n: (0, 0, 0)),
            pl.BlockSpec((CHUNK, groups), lambda b, n: (0, 0)),
        ],
        out_specs=pl.BlockSpec((None, tc, width), lambda b, n: (b, n, 0)),
        out_shape=jax.ShapeDtypeStruct((batch, seq, width), F32),
        compiler_params=_params("parallel", "parallel"),
        name="prompt_gmlp",
    )(z, z, ln_g.reshape(1, width), ln_b.reshape(1, width), w_s, b_s.T)


def _out_proj_kernel(*refs, n_attn, exact):
    a_refs = refs[:n_attn]
    gm_ref, x_ref, ga_ref, gg_ref, w_ref, o_ref, yn_ref = refs[n_attn:]
    wa = a_refs[0].shape[1]

    @pl.when(pl.program_id(1) == 0)
    def _():
        rows = min(x_ref.shape[0], 128)

        def body(c, carry):
            sl = pl.ds(pl.multiple_of(c * rows, rows), rows)
            parts = [a[sl, :] for a in a_refs]
            ssq = functools.reduce(lambda p, q: p + q,
                                   [jnp.sum(p * p, axis=-1, keepdims=True) for p in parts])
            rs = lax.rsqrt(ssq / (n_attn * wa) + EPS)
            for idx, p in enumerate(parts):
                cs = slice(idx * wa, (idx + 1) * wa)
                yn_ref[sl, cs] = (p * rs * ga_ref[:, cs]).astype(yn_ref.dtype)
            gm = gm_ref[sl, :]
            yn_ref[sl, n_attn * wa:] = (gm * _rms_scale(gm) * gg_ref[...]).astype(yn_ref.dtype)
            return carry

        lax.fori_loop(0, x_ref.shape[0] // rows, body, 0)

    o_ref[...] = x_ref[...] + _mm(yn_ref[...], w_ref[...], exact)


def _out_proj(attn_parts, gm, x, g_attn, g_gmlp, w, *, tm, tn, name, exact=False):
    m, d = x.shape
    n_attn = len(attn_parts)
    wa = attn_parts[0].shape[1]
    wg = gm.shape[1]
    part = pl.BlockSpec((tm, wa), lambda i, j: (i, 0))
    return pl.pallas_call(
        functools.partial(_out_proj_kernel, n_attn=n_attn, exact=exact),
        grid=(m // tm, d // tn),
        in_specs=[part] * n_attn + [
            pl.BlockSpec((tm, wg), lambda i, j: (i, 0)),
            pl.BlockSpec((tm, tn), lambda i, j: (i, j)),
            pl.BlockSpec((1, n_attn * wa), lambda i, j: (0, 0)),
            pl.BlockSpec((1, wg), lambda i, j: (0, 0)),
            pl.BlockSpec((n_attn * wa + wg, tn), lambda i, j: (0, j)),
        ],
        out_specs=pl.BlockSpec((tm, tn), lambda i, j: (i, j)),
        out_shape=jax.ShapeDtypeStruct((m, d), F32),
        scratch_shapes=[pltpu.VMEM((tm, n_attn * wa + wg), F32 if exact else BF16)],
        compiler_params=_params("parallel", "arbitrary"),
        name=name,
    )(*attn_parts, gm, x, g_attn.reshape(1, -1), g_gmlp.reshape(1, -1), w)


ROUTE_LANES = 128


def _router_kernel(h_ref, g_ref, wr_ref, br_ref, c0_ref, m_ref, route_ref, cnt_ref, carry_ref):
    tm = h_ref.shape[0]
    n_exp = wr_ref.shape[1]

    @pl.when(pl.program_id(0) == 0)
    def _():
        carry_ref[...] = c0_ref[...]

    h = h_ref[...]
    m = h * _rms_scale(h) * g_ref[...]
    m_ref[...] = m
    logits = _mm(m, wr_ref[...], exact=True) + br_ref[...]

    lane = lax.broadcasted_iota(jnp.int32, (tm, n_exp), 1)
    work = logits
    vals, sels, idxs = [], [], []
    for _ in range(TOP_K):
        mx = jnp.max(work, axis=-1, keepdims=True)
        idx = jnp.min(jnp.where(work == mx, lane, n_exp), axis=-1, keepdims=True)
        sel = lane == idx
        vals.append(mx)
        idxs.append(idx)
        sels.append(sel)
        work = jnp.where(sel, -jnp.inf, work)
    es = [jnp.exp(v - vals[0]) for v in vals]
    inv = 1.0 / functools.reduce(lambda a, b: a + b, es)

    hot = functools.reduce(lambda a, b: a + b, [s.astype(F32) for s in sels])
    ri = lax.broadcasted_iota(jnp.int32, (tm, tm), 0)
    rj = lax.broadcasted_iota(jnp.int32, (tm, tm), 1)
    earlier = (rj < ri).astype(BF16)
    base = jnp.dot(earlier, hot.astype(BF16), preferred_element_type=F32) + carry_ref[...]
    ranks = [jnp.sum(jnp.where(s, base, 0.0), axis=-1, keepdims=True) for s in sels]
    carry_ref[...] = carry_ref[...] + jnp.sum(hot, axis=0, keepdims=True)
    cnt_ref[...] = carry_ref[...]

    ol = lax.broadcasted_iota(jnp.int32, (tm, ROUTE_LANES), 1)
    out = jnp.zeros((tm, ROUTE_LANES), F32)
    for k in range(TOP_K):
        out = jnp.where(ol == k, idxs[k].astype(F32), out)
        out = jnp.where(ol == TOP_K + k, ranks[k], out)
        out = jnp.where(ol == 2 * TOP_K + k, es[k] * inv, out)
    route_ref[...] = out


def _router(h, g, w_r, b_r, counts0, *, tm, name):
    t, d = h.shape
    n_exp = w_r.shape[1]
    return pl.pallas_call(
        _router_kernel,
        grid=(t // tm,),
        in_specs=[
            pl.BlockSpec((tm, d), lambda i: (i, 0)),
            pl.BlockSpec((1, d), lambda i: (0, 0)),
            pl.BlockSpec((d, n_exp), lambda i: (0, 0)),
            pl.BlockSpec((1, n_exp), lambda i: (0, 0)),
            pl.BlockSpec((1, n_exp), lambda i: (0, 0)),
        ],
        out_specs=[
            pl.BlockSpec((tm, d), lambda i: (i, 0)),
            pl.BlockSpec((tm, ROUTE_LANES), lambda i: (i, 0)),
            pl.BlockSpec((1, n_exp), lambda i: (0, 0)),
        ],
        out_shape=[
            jax.ShapeDtypeStruct((t, d), F32),
            jax.ShapeDtypeStruct((t, ROUTE_LANES), F32),
            jax.ShapeDtypeStruct((1, n_exp), F32),
        ],
        scratch_shapes=[pltpu.VMEM((1, n_exp), F32)],
        compiler_params=_params("arbitrary"),
        name=name,
    )(h, g.reshape(1, d), w_r, b_r.reshape(1, n_exp), counts0)


def _scatter_kernel(*refs, zero_fill, n_exp):
    if zero_fill:
        pstart_ref, pend_ref, dest_ref, m_ref, rows_hbm, zeros_ref, sem = refs
    else:
        pstart_ref, pend_ref, dest_ref, m_ref, _, rows_hbm, zeros_ref, sem = refs
    ts = dest_ref.shape[0] // TOP_K
    i = pl.program_id(0)

    if zero_fill:
        @pl.when(i == 0)
        def _():
            zeros_ref[...] = jnp.zeros_like(zeros_ref)

            def fill(e):
                return pltpu.make_async_copy(
                    zeros_ref, rows_hbm.at[pl.ds(pl.multiple_of(pend_ref[e] - MOE_CHUNK, MOE_CHUNK), MOE_CHUNK)], sem)

            def start(e, carry):
                @pl.when(pend_ref[e] > pstart_ref[e])
                def _():
                    fill(e).start()
                return carry

            def wait(e, carry):
                @pl.when(pend_ref[e] > pstart_ref[e])
                def _():
                    fill(e).wait()
                return carry

            lax.fori_loop(0, n_exp, start, 0)
            lax.fori_loop(0, n_exp, wait, 0)

    def row_copy(src, dst):
        return pltpu.make_async_copy(m_ref.at[pl.ds(src, 1)], rows_hbm.at[pl.ds(dst, 1)], sem)

    def start(t, carry):
        for k in range(TOP_K):
            row_copy(t, dest_ref[TOP_K * t + k]).start(priority=k % 2)
        return carry

    def wait(t, carry):
        for k in range(TOP_K):
            row_copy(0, 0).wait()
        return carry

    unroll = min(ts, 8)
    lax.fori_loop(0, ts, start, 0, unroll=unroll)
    lax.fori_loop(0, ts, wait, 0, unroll=unroll)


def _scatter_rows(m, dest, pstart, pend, rows_in, *, ts, n_rows, name):
    t, d = m.shape
    zero_fill = rows_in is None
    n_exp = pstart.shape[0]
    in_specs = [
        pl.BlockSpec((ts * TOP_K,), lambda i, *_: (i,), memory_space=pltpu.SMEM),
        pl.BlockSpec((ts, d), lambda i, *_: (i, 0)),
    ]
    args = [dest, m]
    aliases = {}
    if not zero_fill:
        in_specs.append(pl.BlockSpec(memory_space=pl.ANY))
        args.append(rows_in)
        aliases = {4: 0}
    return pl.pallas_call(
        functools.partial(_scatter_kernel, zero_fill=zero_fill, n_exp=n_exp),
        grid_spec=pltpu.PrefetchScalarGridSpec(
            num_scalar_prefetch=2,
            grid=(t // ts,),
            in_specs=in_specs,
            out_specs=pl.BlockSpec(memory_space=pl.ANY),
            scratch_shapes=[pltpu.VMEM((MOE_CHUNK, d), F32), pltpu.SemaphoreType.DMA(())],
        ),
        out_shape=jax.ShapeDtypeStruct((n_rows, d), F32),
        input_output_aliases=aliases,
        compiler_params=_params("arbitrary"),
        name=name,
    )(pstart, pend, *args)


def _expert_kernel(ir_ref, inch_ref, ie_ref, nv_ref,
                   rows_hbm, wgu_hbm, wd_hbm, bg_ref, bl_ref, bd_ref, ys_hbm,
                   stage, xb, hdn, ybuf, wring, sem_in, sem_out, sem_w, *, n_ff, n_dt):
    i = pl.program_id(0)
    s = pl.program_id(1)
    n_items = pl.num_programs(0)
    n_p1 = n_ff // MOE_P1_TILES
    n_p2 = n_dt // MOE_P2_TILES
    n_steps = n_p1 + n_p2
    nch = inch_ref[i]
    tile = ybuf.shape[2]
    d_ff = n_ff * tile
    step_id = i * n_steps + s
    live_steps = nv_ref[0] * n_steps
    slot = lax.rem(step_id, WEIGHT_RING)

    def weight_copies(item, step, ring_slot, first_phase):
        e = ie_ref[item]

        def tile_copy(src, col, k):
            return pltpu.make_async_copy(src.at[e, :, pl.ds(pl.multiple_of(col, tile), tile)],
                                         wring.at[ring_slot, k], sem_w.at[ring_slot, k])

        if first_phase:
            cps = []
            for q in range(MOE_P1_TILES):
                col = (step * MOE_P1_TILES + q) * tile
                cps += [tile_copy(wgu_hbm, col, 2 * q), tile_copy(wgu_hbm, d_ff + col, 2 * q + 1)]
            return cps
        col = (step - n_p1) * (MOE_P2_TILES * tile)
        return [tile_copy(wd_hbm, col + k * tile, k) for k in range(MOE_P2_TILES)]

    def weights(op, item, step, ring_slot):
        @pl.when(step < n_p1)
        def _():
            for cp in weight_copies(item, step, ring_slot, True):
                op(cp)

        @pl.when(step >= n_p1)
        def _():
            for cp in weight_copies(item, step, ring_slot, False):
                op(cp)

    def fetch(ahead):
        tgt = step_id + ahead

        @pl.when(tgt < live_steps)
        def _():
            weights(lambda cp: cp.start(), lax.div(tgt, n_steps), lax.rem(tgt, n_steps),
                    lax.rem(tgt, WEIGHT_RING))

    @pl.when(step_id == 0)
    def _():
        for ahead in range(WEIGHT_RING - 1):
            fetch(ahead)

    fetch(WEIGHT_RING - 1)

    def chunk(c):
        return slice(c * MOE_CHUNK, (c + 1) * MOE_CHUNK)

    def row_window(item, c):
        return pl.ds(pl.multiple_of(ir_ref[item], MOE_CHUNK) + c * MOE_CHUNK, MOE_CHUNK)

    def load(item, c):
        return pltpu.make_async_copy(rows_hbm.at[row_window(item, c)], stage.at[chunk(c)], sem_in.at[c])

    def store(item, c, n):
        return pltpu.make_async_copy(
            ybuf.at[n, chunk(c)], ys_hbm.at[row_window(item, c), pl.ds(n * tile, tile)], sem_out.at[c])

    def for_chunks(count, fn):
        for c in range(MOE_MAX_CHUNKS):
            pl.when(c < count)(functools.partial(fn, c))

    def for_sizes(fn):
        for nc in range(1, MOE_MAX_CHUNKS + 1):
            pl.when(nch == nc)(functools.partial(fn, nc * MOE_CHUNK))

    @pl.when(s == 0)
    def _():
        @pl.when(i == 0)
        def _():
            for_chunks(nch, lambda c: load(0, c).start())

        def land(c):
            load(i, c).wait()
            xb[chunk(c), :] = stage[chunk(c), :].astype(BF16)

        for_chunks(nch, land)

    @pl.when(nch > 0)
    def _():
        weights(lambda cp: cp.wait(), i, s, slot)

    @pl.when((s < n_p1) & (nch > 0))
    def _():
        def phase1(rows):
            x = xb[:rows, :]
            for q in range(MOE_P1_TILES):
                wg = wring[slot, 2 * q].astype(BF16)
                wl = wring[slot, 2 * q + 1].astype(BF16)
                cols = slice(q * tile, (q + 1) * tile)
                gate = jnp.dot(x, wg, preferred_element_type=F32) + bg_ref[:, cols]
                lin = jnp.dot(x, wl, preferred_element_type=F32) + bl_ref[:, cols]
                gate = jnp.minimum(gate, SWIGLU_LIMIT)
                lin = jnp.clip(lin, -SWIGLU_LIMIT, SWIGLU_LIMIT)
                act = gate * jax.nn.sigmoid(SWIGLU_ALPHA * gate) * (lin + 1.0)
                hdn[s * MOE_P1_TILES + q, :rows, :] = act.astype(BF16)

        for_sizes(phase1)

    @pl.when(s == n_p1)
    def _():
        @pl.when(i > 0)
        def _():
            def drain(c):
                for n in range(n_dt):
                    store(i - 1, c, n).wait()

            for_chunks(inch_ref[jnp.maximum(i - 1, 0)], drain)

        @pl.when(i + 1 < n_items)
        def _():
            nxt = jnp.minimum(i + 1, n_items - 1)
            for_chunks(inch_ref[nxt], lambda c: load(nxt, c).start())

    @pl.when((s >= n_p1) & (nch > 0))
    def _():
        n = s - n_p1

        def phase2(rows):
            for k in range(MOE_P2_TILES):
                wd = wring[slot, k].astype(BF16)
                y = bd_ref[:, k * tile:(k + 1) * tile]
                for f in range(n_ff):
                    y = y + jnp.dot(hdn[f, :rows, :], wd[f * tile:(f + 1) * tile, :],
                                    preferred_element_type=F32)
                ybuf[MOE_P2_TILES * n + k, :rows, :] = y

        for_sizes(phase2)

    @pl.when(s == n_steps - 1)
    def _():
        def flush(c):
            for n in range(n_dt):
                store(i, c, n).start()

        for_chunks(nch, flush)

        @pl.when(i == n_items - 1)
        def _():
            def drain(c):
                for n in range(n_dt):
                    store(i, c, n).wait()

            for_chunks(nch, drain)


def _expert_ffn(rows, w_gu, b_gu, w_down, b_down, item_e, item_row0, item_nch):
    n_rows, d = rows.shape
    n_exp, _, two_ff = w_gu.shape
    d_ff = two_ff // 2
    tile = MOE_FF_TILE
    n_ff = d_ff // tile
    n_dt = d // tile
    n_p1, n_p2 = n_ff // MOE_P1_TILES, n_dt // MOE_P2_TILES
    n_steps = n_p1 + n_p2
    n_items = item_e.shape[0]
    r_max = MOE_MAX_CHUNKS * MOE_CHUNK
    sub_slots = max(2 * MOE_P1_TILES, MOE_P2_TILES)

    assert d == d_ff, "one ring slot shape serves both phases' weight tiles"
    n_live = jnp.sum((item_nch > 0).astype(jnp.int32)).reshape(1)

    def gate_map(i, s, ir, inch, ie, nv):
        return ie[i], 0, jnp.minimum(s, n_p1 - 1)

    def lin_map(i, s, ir, inch, ie, nv):
        return ie[i], 0, n_p1 + jnp.minimum(s, n_p1 - 1)

    def down_map(i, s, ir, inch, ie, nv):
        return ie[i], 0, jnp.maximum(s - n_p1, 0)

    return pl.pallas_call(
        functools.partial(_expert_kernel, n_ff=n_ff, n_dt=n_dt),
        grid_spec=pltpu.PrefetchScalarGridSpec(
            num_scalar_prefetch=4,
            grid=(n_items, n_steps),
            in_specs=[
                pl.BlockSpec(memory_space=pl.ANY),
                pl.BlockSpec(memory_space=pl.ANY),
                pl.BlockSpec(memory_space=pl.ANY),
                pl.BlockSpec((None, 1, MOE_P1_TILES * tile), gate_map),
                pl.BlockSpec((None, 1, MOE_P1_TILES * tile), lin_map),
                pl.BlockSpec((None, 1, MOE_P2_TILES * tile), down_map),
            ],
            out_specs=pl.BlockSpec(memory_space=pl.ANY),
            scratch_shapes=[
                pltpu.VMEM((r_max, d), F32),
                pltpu.VMEM((r_max, d), BF16),
                pltpu.VMEM((n_ff, r_max, tile), BF16),
                pltpu.VMEM((n_dt, r_max, tile), F32),
                pltpu.VMEM((WEIGHT_RING, sub_slots, d, tile), F32),
                pltpu.SemaphoreType.DMA((MOE_MAX_CHUNKS,)),
                pltpu.SemaphoreType.DMA((MOE_MAX_CHUNKS,)),
                pltpu.SemaphoreType.DMA((WEIGHT_RING, sub_slots)),
            ],
        ),
        out_shape=jax.ShapeDtypeStruct((n_rows, d), F32),
        compiler_params=_params("arbitrary", "arbitrary"),
        name="expert_ffn",
    )(item_row0, item_nch, item_e.astype(jnp.int32), n_live, rows, w_gu, w_down,
      b_gu.reshape(n_exp, 1, two_ff), b_gu.reshape(n_exp, 1, two_ff),
      b_down.reshape(n_exp, 1, d))


def _combine_kernel(dest_ref, route_ref, h_ref, ys_hbm, o_ref, buf, sem):
    tg = h_ref.shape[0]

    def row_copy(src, k, t):
        return pltpu.make_async_copy(ys_hbm.at[pl.ds(src, 1)], buf.at[k, pl.ds(t, 1)], sem)

    def start(t, carry):
        for k in range(TOP_K):
            row_copy(dest_ref[TOP_K * t + k], k, t).start(priority=k % 2)
        return carry

    def wait(t, carry):
        for k in range(TOP_K):
            row_copy(0, k, 0).wait()
        return carry

    unroll = min(tg, 8)
    lax.fori_loop(0, tg, start, 0, unroll=unroll)
    lax.fori_loop(0, tg, wait, 0, unroll=unroll)
    out = h_ref[...]
    for k in range(TOP_K):
        out = out + route_ref[:, 2 * TOP_K + k:2 * TOP_K + k + 1] * buf[k]
    o_ref[...] = out


def _combine(dest, route, h, ys, *, tg, name):
    t, d = h.shape
    return pl.pallas_call(
        _combine_kernel,
        grid=(t // tg,),
        in_specs=[
            pl.BlockSpec((tg * TOP_K,), lambda i: (i,), memory_space=pltpu.SMEM),
            pl.BlockSpec((tg, ROUTE_LANES), lambda i: (i, 0)),
            pl.BlockSpec((tg, d), lambda i: (i, 0)),
            pl.BlockSpec(memory_space=pl.ANY),
        ],
        out_specs=pl.BlockSpec((tg, d), lambda i: (i, 0)),
        out_shape=jax.ShapeDtypeStruct((t, d), F32),
        scratch_shapes=[pltpu.VMEM((TOP_K, tg, d), F32), pltpu.SemaphoreType.DMA(())],
        compiler_params=_params("arbitrary"),
        name=name,
    )(dest, route, h, ys)


def _ple_kernel(h_ref, p_ref, g_ref, wg_ref, wp_ref, o_ref, hn_ref):
    rows = min(h_ref.shape[0], 128)

    def body(c, carry):
        sl = pl.ds(pl.multiple_of(c * rows, rows), rows)
        h = h_ref[sl, :]
        hn_ref[sl, :] = (h * _rms_scale(h) * g_ref[...]).astype(BF16)
        return carry

    lax.fori_loop(0, h_ref.shape[0] // rows, body, 0)
    gate = jax.nn.sigmoid(jnp.dot(hn_ref[...], wg_ref[...].astype(BF16), preferred_element_type=F32))
    proj = jnp.dot(p_ref[...].astype(BF16), wp_ref[...].astype(BF16), preferred_element_type=F32)
    o_ref[...] = h_ref[...] + gate * proj


def _ple(h, p, g, w_gate, w_proj, *, tm, name):
    t, d = h.shape
    pd = p.shape[1]
    return pl.pallas_call(
        _ple_kernel,
        grid=(t // tm,),
        in_specs=[
            pl.BlockSpec((tm, d), lambda i: (i, 0)),
            pl.BlockSpec((tm, pd), lambda i: (i, 0)),
            pl.BlockSpec((1, d), lambda i: (0, 0)),
            pl.BlockSpec((d, d), lambda i: (0, 0)),
            pl.BlockSpec((pd, d), lambda i: (0, 0)),
        ],
        out_specs=pl.BlockSpec((tm, d), lambda i: (i, 0)),
        out_shape=jax.ShapeDtypeStruct((t, d), F32),
        scratch_shapes=[pltpu.VMEM((tm, d), BF16)],
        compiler_params=_params("parallel"),
        name=name,
    )(h, p, g.reshape(1, d), w_gate, w_proj)


def _decode_kernel(*refs, groups, hpg):
    ng = len(groups)
    z_ref = refs[0]
    caches = refs[1:1 + ng]
    (cos_ref, sa_ref, sb_ref, gq_ref, gk_ref, lng_ref, lnb_ref, ws0_ref, bs0_ref,
     attn_ref, kn_ref, gm_ref, vn_ref) = refs[1 + ng:]
    heads = ng * hpg
    aw = heads * HEAD_DIM
    scale = HEAD_DIM ** -0.5
    tabs = (cos_ref[...], sa_ref[...], sb_ref[...])

    def head(base, hh):
        return z_ref[:, base + hh * HEAD_DIM:base + (hh + 1) * HEAD_DIM]

    outs, lses = [], []
    for g in range(ng):
        c_ref = caches[g]
        for j in range(hpg):
            hh = g * hpg + j
            qn = _norm_rope(head(0, hh), gq_ref[...], *tabs) * scale
            kn = _norm_rope(head(aw, hh), gk_ref[...], *tabs)
            vv = head(2 * aw, hh)
            kn_ref[:, hh * HEAD_DIM:(hh + 1) * HEAD_DIM] = kn
            kc = c_ref[:, 0, j, :]
            vc = c_ref[:, 1, j, :]
            s_c = jnp.sum(kc * qn, axis=-1, keepdims=True)
            s_n = jnp.sum(kn * qn, axis=-1, keepdims=True)
            m = jnp.maximum(jnp.max(s_c, axis=0, keepdims=True), s_n)
            p_c = jnp.exp(s_c - m)
            p_n = jnp.exp(s_n - m)
            den = jnp.sum(p_c, axis=0, keepdims=True) + p_n
            o = (jnp.sum(p_c * vc, axis=0, keepdims=True) + p_n * vv) / den
            outs.append(o)
            lses.append(m + jnp.log(den))

    for j in range(hpg):
        ls = [lses[g * hpg + j] for g in range(ng)]
        mx = functools.reduce(jnp.maximum, ls)
        es = [jnp.exp(l - mx) for l in ls]
        inv = 1.0 / functools.reduce(lambda a, b: a + b, es)
        for g in range(ng):
            hh = g * hpg + j
            attn_ref[:, hh * HEAD_DIM:(hh + 1) * HEAD_DIM] = outs[hh] * (es[g] * inv)

    gw = gm_ref.shape[1]
    for g in range(gw // GMLP_CH):
        cols = slice(g * GMLP_CH, (g + 1) * GMLP_CH)
        v = _gelu(z_ref[:, 3 * aw + gw + g * GMLP_CH:3 * aw + gw + (g + 1) * GMLP_CH])
        mu = jnp.mean(v, axis=-1, keepdims=True)
        vc = v - mu
        var = jnp.mean(vc * vc, axis=-1, keepdims=True)
        vn = vc * lax.rsqrt(var + EPS) * lng_ref[:, cols] + lnb_ref[:, cols]
        vn_ref[:, cols] = vn
        u = _gelu(z_ref[:, 3 * aw + g * GMLP_CH:3 * aw + (g + 1) * GMLP_CH])
        gm_ref[:, cols] = u * (ws0_ref[:, cols] * vn + bs0_ref[:, cols])


def _sample_mixers(zs, caches, tables, g_q, g_k, ln_g, ln_b, w_s, b_s, *, hpg):
    bsz, in_width = zs.shape
    ng = len(DILATION_GROUPS)
    aw = ng * hpg * HEAD_DIM
    gw = (in_width - 3 * aw) // 2
    in_specs = [pl.BlockSpec((None, 1, in_width), lambda b: (b, 0, 0))]
    cache_args = []
    for (window, dil), c in zip(DILATION_GROUPS, caches):
        wc = c.shape[-4]
        assert wc == window and c.shape[-5] == bsz, "window caches must hold a full window"
        steps = wc // dil
        cache_args.append(c.reshape(bsz, steps, dil, 2, hpg, HEAD_DIM))
        in_specs.append(pl.BlockSpec((None, steps, None, 2, hpg, HEAD_DIM),
                                     lambda b: (b, 0, 0, 0, 0, 0)))
    vec = pl.BlockSpec((1, HEAD_DIM), lambda b: (0, 0))
    wide = pl.BlockSpec((1, gw), lambda b: (0, 0))
    in_specs += [vec] * 5 + [wide] * 4

    def out(w):
        return pl.BlockSpec((None, 1, w), lambda b: (b, 0, 0)), jax.ShapeDtypeStruct((bsz, 1, w), F32)

    specs, shapes = zip(out(aw), out(aw), out(gw), out(gw))
    groups = w_s.shape[0]
    ws0 = jnp.repeat(w_s[:, 0, 0], GMLP_CH).reshape(1, gw)
    bs0 = jnp.repeat(b_s[:, 0], GMLP_CH).reshape(1, gw)
    assert groups * GMLP_CH == gw
    res = pl.pallas_call(
        functools.partial(_decode_kernel, groups=DILATION_GROUPS, hpg=hpg),
        grid=(bsz,),
        in_specs=in_specs,
        out_specs=list(specs),
        out_shape=list(shapes),
        compiler_params=_params("parallel"),
        name="sample_mixers",
    )(zs.reshape(bsz, 1, in_width), *cache_args, *tables,
      g_q.reshape(1, HEAD_DIM), g_k.reshape(1, HEAD_DIM),
      ln_g.reshape(1, gw), ln_b.reshape(1, gw), ws0, bs0)
    return [r.reshape(bsz, -1) for r in res]


def _expert_layout(counts, n_items):
    counts = counts.astype(jnp.int32)
    padded = (counts + MOE_CHUNK - 1) // MOE_CHUNK * MOE_CHUNK
    pend = jnp.cumsum(padded)
    pstart = pend - padded
    r_max = MOE_MAX_CHUNKS * MOE_CHUNK
    per_e = (padded + r_max - 1) // r_max
    item_end = jnp.cumsum(per_e)
    total = item_end[-1]
    ids = jnp.arange(n_items, dtype=jnp.int32)
    clamped = jnp.minimum(ids, total - 1)
    e = jnp.sum((item_end[None, :] <= clamped[:, None]).astype(jnp.int32), axis=1)
    local = clamped - (item_end[e] - per_e[e])
    row0 = pstart[e] + local * r_max
    nch = jnp.minimum(padded[e] - local * r_max, r_max) // MOE_CHUNK
    nch = jnp.where(ids < total, nch, 0)
    return pstart, pend, e, row0.astype(jnp.int32), nch.astype(jnp.int32)


def _destinations(route, pstart):
    e = route[:, :TOP_K].astype(jnp.int32)
    rank = route[:, TOP_K:2 * TOP_K].astype(jnp.int32)
    return (pstart[e] + rank).reshape(-1)


def kernel(x_prompt, x_sample, cache_kv_w128, cache_kv_w512, cache_kv_w2048, p_prompt, p_sample,
           g_mix, w_in, g_q, g_k, gmlp_ln_g, gmlp_ln_b, gmlp_w_s, gmlp_b_s, g_out_attn, g_out_gmlp,
           w_out, g_ffn, w_router, b_router, w_gu, b_gu, w_down, b_down, g_ple, w_ple_gate,
           w_ple_proj):
    depth = g_mix.shape[0]
    assert depth == 1, "single-layer step"
    batch, seq, d = x_prompt.shape
    dec_batch, dec_seq, _ = x_sample.shape
    assert dec_seq == 1
    in_width = w_in.shape[2]
    gw = gmlp_ln_g.shape[1] * gmlp_ln_g.shape[2]
    aw = (in_width - 2 * gw) // 3
    ng = len(DILATION_GROUPS)
    hpg = aw // HEAD_DIM // ng
    n_exp = w_router.shape[2]
    tp = batch * seq
    ts = dec_batch * dec_seq

    def layer(a):
        return a.reshape(a.shape[1:])

    caches = (cache_kv_w128, cache_kv_w512, cache_kv_w2048)
    (p_prompt, p_sample, g_mix, w_in, g_q, g_k, gmlp_ln_g, gmlp_ln_b, gmlp_w_s, gmlp_b_s,
     g_out_attn, g_out_gmlp, w_out, g_ffn, w_router, b_router, w_gu, b_gu, w_down, b_down, g_ple,
     w_ple_gate, w_ple_proj) = [layer(a) for a in (
         p_prompt, p_sample, g_mix, w_in, g_q, g_k, gmlp_ln_g, gmlp_ln_b, gmlp_w_s, gmlp_b_s,
         g_out_attn, g_out_gmlp, w_out, g_ffn, w_router, b_router, w_gu, b_gu, w_down, b_down,
         g_ple, w_ple_gate, w_ple_proj)]

    w_in_b, w_out_b = w_in.astype(BF16), w_out.astype(BF16)
    w_ple_gate_b, w_ple_proj_b = w_ple_gate.astype(BF16), w_ple_proj.astype(BF16)
    xp = x_prompt.reshape(tp, d)
    z = _norm_matmul(xp, g_mix, w_in_b, tm=512, tn=in_width // 2, name="prompt_in_proj",
                     rows_inner=True)
    z3 = z.reshape(batch, seq, in_width)
    tables_p = _rope_tables(jnp.arange(seq, dtype=jnp.int32))
    attn_p, kn_p = _prompt_attention(z3, g_q, g_k, tables_p, batch=batch, seq=seq, hpg=hpg)
    gm_p = _prompt_gmlp(z3, gmlp_ln_g, gmlp_ln_b, gmlp_w_s, gmlp_b_s,
                        batch=batch, seq=seq, width=gw, col_block=3 * aw // gw)
    hp = _out_proj([a.reshape(tp, -1) for a in attn_p], gm_p.reshape(tp, gw), xp,
                   g_out_attn, g_out_gmlp, w_out_b, tm=512, tn=d, name="prompt_out_proj")

    xs = x_sample.reshape(ts, d)
    zs = _norm_matmul(xs, g_mix, w_in, tm=ts, tn=512, name="sample_in_proj", exact=True)
    tables_s = _rope_tables(PAST_LEN + jnp.arange(dec_seq, dtype=jnp.int32))
    attn_s, kn_s, gm_s, vn_s = _sample_mixers(
        zs, caches, tables_s, g_q, g_k, gmlp_ln_g, gmlp_ln_b, gmlp_w_s, gmlp_b_s, hpg=hpg)
    gwid = hpg * HEAD_DIM
    hs = _out_proj([attn_s[:, g * gwid:(g + 1) * gwid] for g in range(ng)], gm_s, xs,
                   g_out_attn, g_out_gmlp, w_out, tm=ts, tn=512, name="sample_out_proj",
                   exact=True)

    zero_counts = jnp.zeros((1, n_exp), F32)
    m_p, route_p, counts_p = _router(hp, g_ffn, w_router, b_router, zero_counts,
                                     tm=512, name="prompt_router")
    m_s, route_s, counts = _router(hs, g_ffn, w_router, b_router, counts_p,
                                   tm=ts, name="sample_router")
    n_assign = (tp + ts) * TOP_K
    n_chunks = -(-n_assign // MOE_CHUNK) + n_exp
    n_rows = n_chunks * MOE_CHUNK
    n_items = n_exp + -(-n_chunks // MOE_MAX_CHUNKS)
    pstart, pend, item_e, item_row0, item_nch = _expert_layout(counts.reshape(n_exp), n_items)
    dest_p = _destinations(route_p, pstart)
    dest_s = _destinations(route_s, pstart)
    rows = _scatter_rows(m_p, dest_p, pstart, pend, None, ts=512, n_rows=n_rows,
                         name="prompt_scatter")
    rows = _scatter_rows(m_s, dest_s, pstart, pend, rows, ts=ts, n_rows=n_rows,
                         name="sample_scatter")
    ys = _expert_ffn(rows, w_gu, b_gu, w_down, b_down, item_e, item_row0, item_nch)
    hp = _combine(dest_p, route_p, hp, ys, tg=256, name="prompt_combine")
    hs = _combine(dest_s, route_s, hs, ys, tg=ts, name="sample_combine")

    yp = _ple(hp, p_prompt.reshape(tp, -1), g_ple, w_ple_gate_b, w_ple_proj_b,
              tm=512, name="prompt_ple")
    ysm = _ple(hs, p_sample.reshape(ts, -1), g_ple, w_ple_gate_b, w_ple_proj_b,
               tm=ts, name="sample_ple")

    v0 = 2 * aw
    kv_prompt, kv_sample = [], []
    for g, (window, _) in enumerate(DILATION_GROUPS):
        wc = min(window, seq)
        kg = kn_p[g][:, seq - wc:].reshape(batch, wc, hpg, HEAD_DIM)
        vg = z3[:, seq - wc:, v0 + g * gwid:v0 + (g + 1) * gwid].reshape(batch, wc, hpg, HEAD_DIM)
        kv_prompt.append(jnp.stack([kg, vg], axis=2)[None])
        ksg = kn_s[:, g * gwid:(g + 1) * gwid].reshape(dec_batch, dec_seq, hpg, HEAD_DIM)
        vsg = zs[:, v0 + g * gwid:v0 + (g + 1) * gwid].reshape(dec_batch, dec_seq, hpg, HEAD_DIM)
        kv_sample.append(jnp.stack([ksg, vsg], axis=2)[None])
    gmlp_v = vn_s.reshape(1, dec_batch, dec_seq, gw // GMLP_CH, GMLP_CH)
    return (yp.reshape(batch, seq, d), ysm.reshape(dec_batch, dec_seq, d),
            *kv_prompt, *kv_sample, gmlp_v)
```
